```python
import math
import jax, jax.numpy as jnp
from jax import lax
import numpy as np

D_MODEL = 1024
BATCH = 2
SEQ = 8192
DEPTH = 2
DEC_BATCH = 16
DEC_SEQ = 2048
PAST_LEN = 128

GRID_W = 64
NA_HEADS = 8
NA_HEAD_DIM = 64
D_ATTN = NA_HEADS * NA_HEAD_DIM
NA_WIN_ROWS = 8
NA_WIN_COLS = 16
NA_QBLOCK_COLS = 16
NA_KBLOCK_COLS = 32
D_HYENA = D_MODEL // 2
SHORT_CONV = 3
FILTER_EMB = 33
FILTER_HIDDEN = 64
DECAY_FAST = 0.3
DECAY_SLOW = 1.5
DECAY_TARGET = 1e-2
FILTER_OUT_STD = 0.005
N_GROUPS = 4
EXPERTS_PER_GROUP = 4
N_EXPERTS = N_GROUPS * EXPERTS_PER_GROUP
TOP_K = 2
D_EXPERT = 512
LN_EPS = 1e-5
ALPHA = (2.0 * DEPTH) ** 0.25
BETA = (8.0 * DEPTH) ** -0.25
D_IN = 3 * D_ATTN + 3 * D_HYENA + 2 * D_MODEL
SPLITS = (D_ATTN, 2 * D_ATTN, 3 * D_ATTN, 3 * D_ATTN + 3 * D_HYENA, 3 * D_ATTN + 3 * D_HYENA + D_MODEL)

kernel_name = "hybrid_na_hyena_hmoe_encoder"


def layer_norm(x, g, b):
    xf = x.astype(jnp.float32)
    mu = jnp.mean(xf, axis=-1, keepdims=True)
    var = jnp.mean(jnp.square(xf - mu), axis=-1, keepdims=True)
    y = (xf - mu) * lax.rsqrt(var + LN_EPS) * g.astype(jnp.float32) + b.astype(jnp.float32)
    return y.astype(x.dtype)


def _na_indices(rows):
    kh = min(NA_WIN_ROWS, rows)
    r = np.arange(rows)
    row_start = np.clip(r - kh // 2, 0, rows - kh)
    row_off = row_start[:, None] + np.arange(kh)[None, :] - r[:, None] + (NA_WIN_ROWS - 1)
    n_cb = GRID_W // NA_QBLOCK_COLS
    qcol = np.arange(GRID_W).reshape(n_cb, NA_QBLOCK_COLS)
    col_start = np.clip(qcol - NA_WIN_COLS // 2, 0, GRID_W - NA_WIN_COLS)
    kcol = np.clip(qcol[:, :1] - NA_WIN_COLS // 2, 0, GRID_W - NA_KBLOCK_COLS) + np.arange(NA_KBLOCK_COLS)[None, :]
    valid = (kcol[:, None, :] >= col_start[..., None]) & (kcol[:, None, :] < col_start[..., None] + NA_WIN_COLS)
    col_off = np.clip(kcol[:, None, :] - qcol[:, :, None], 1 - NA_WIN_COLS, NA_WIN_COLS - 1) + (NA_WIN_COLS - 1)
    return kh, row_start.astype(np.int32), row_off.astype(np.int32), kcol, valid, col_off


def neighbourhood_attention(q, k, v, rpb):
    b, l = q.shape[0], q.shape[1]
    rows = l // GRID_W
    kh, row_start, row_off, kcol, valid, col_off = _na_indices(rows)
    n_cb = GRID_W // NA_QBLOCK_COLS
    q_blk = q.reshape(b, rows, n_cb, NA_QBLOCK_COLS, NA_HEADS, NA_HEAD_DIM)
    k_grid = k.reshape(b, rows, GRID_W, NA_HEADS, NA_HEAD_DIM)
    v_grid = v.reshape(b, rows, GRID_W, NA_HEADS, NA_HEAD_DIM)
    bias_cols = rpb.astype(jnp.float32)[:, :, col_off]
    mask = jnp.where(jnp.asarray(valid), 0.0, -1e30).astype(jnp.float32)[:, :, None, :]
    scale = NA_HEAD_DIM ** -0.5

    def row_block(args):
        q_row, r0, r_off = args
        k_blk = lax.dynamic_slice_in_dim(k_grid, r0, kh, axis=1)[:, :, kcol]
        v_blk = lax.dynamic_slice_in_dim(v_grid, r0, kh, axis=1)[:, :, kcol]
        s = jnp.einsum('bnqhd,bmnkhd->bhnqmk', q_row, k_blk, preferred_element_type=jnp.float32) * scale
        bias = jnp.transpose(bias_cols[:, r_off], (0, 2, 3, 1, 4))
        s = s + bias + mask
        p = jax.nn.softmax(s.reshape(s.shape[:4] + (kh * NA_KBLOCK_COLS,)), axis=-1).reshape(s.shape)
        return jnp.einsum('bhnqmk,bmnkhd->bnqhd', p.astype(v_blk.dtype), v_blk)

    out = lax.map(row_block, (jnp.moveaxis(q_blk, 1, 0), jnp.asarray(row_start), jnp.asarray(row_off)))
    return jnp.moveaxis(out, 0, 1).reshape(b, l, D_ATTN)


def implicit_filters(l, w1, b1, w2, b2, w3, b3, w4, freq):
    t = jnp.linspace(0.0, 1.0, l, dtype=jnp.float32)[:, None]
    bands = (FILTER_EMB - 1) // 2
    f = jnp.linspace(1e-4, bands - 1, bands, dtype=jnp.float32)[None, :]
    ang = f * (2.0 * math.pi / l) * jnp.arange(l, dtype=jnp.float32)[:, None]
    z = jnp.concatenate([t, jnp.cos(ang), -jnp.sin(ang)], axis=-1)
    freq = freq.astype(jnp.float32)
    h = jnp.sin(freq * (z @ w1 + b1))
    h = jnp.sin(freq * (h @ w2 + b2))
    h = jnp.sin(freq * (h @ w3 + b3))
    h = (h @ w4).astype(jnp.float32)
    max_decay = math.log(DECAY_TARGET) / DECAY_FAST
    min_decay = math.log(DECAY_TARGET) / DECAY_SLOW
    deltas = jnp.abs(jnp.linspace(min_decay, max_decay, D_HYENA, dtype=jnp.float32))
    decay = jnp.exp(-t * deltas)
    return h * jnp.concatenate([decay, decay], axis=-1)


def bidirectional_long_conv(u, h_fwd, h_bwd, bias):
    l, c = u.shape[1], u.shape[2]
    kern = jnp.concatenate([h_fwd, jnp.zeros((1, c), jnp.float32), h_bwd[1:][::-1]], axis=0)
    k_f = jnp.fft.rfft(kern, n=2 * l, axis=0)
    uf = u.astype(jnp.float32)
    u_f = jnp.fft.rfft(uf, n=2 * l, axis=1)
    y = jnp.fft.irfft(u_f * k_f, n=2 * l, axis=1)[:, :l]
    return (y + uf * bias.astype(jnp.float32)).astype(u.dtype)


def hyena_order2(u, short_w, short_b, flt_w1, flt_b1, flt_w2, flt_b2, flt_w3, flt_b3, flt_w4, flt_freq, flt_bias):
    l, c3 = u.shape[1], u.shape[2]
    u = lax.conv_general_dilated(u, short_w[:, None, :], window_strides=(1,),
                                 padding=[(SHORT_CONV // 2, SHORT_CONV // 2)],
                                 dimension_numbers=('NWC', 'WIO', 'NWC'),
                                 feature_group_count=c3) + short_b
    x0, x1, v = jnp.split(u, 3, axis=-1)
    h = implicit_filters(l, flt_w1, flt_b1, flt_w2, flt_b2, flt_w3, flt_b3, flt_w4, flt_freq)
    y = bidirectional_long_conv(v * x1, h[:, :D_HYENA], h[:, D_HYENA:], flt_bias)
    return y * x0


def token_mixer(x, w_in, b_in, short_w, short_b, na_rpb, flt_w1, flt_b1, flt_w2, flt_b2, flt_w3, flt_b3,
                flt_w4, flt_freq, flt_bias, w_branch_attn, w_branch_hyena, w_out):
    proj = jnp.einsum('bld,de->ble', x, w_in) + b_in
    q, k, v, hy, g_attn, g_hyena = jnp.split(proj, list(SPLITS), axis=-1)
    o_attn = neighbourhood_attention(q, k, v, na_rpb)
    o_hyena = hyena_order2(hy, short_w, short_b, flt_w1, flt_b1, flt_w2, flt_b2, flt_w3, flt_b3,
                           flt_w4, flt_freq, flt_bias)
    merged = (jax.nn.sigmoid(g_attn) * (o_attn @ w_branch_attn)
              + jax.nn.sigmoid(g_hyena) * (o_hyena @ w_branch_hyena))
    return merged @ w_out


def hierarchical_moe(x, w_group, b_group, w_router, b_router, w1, w3, w2):
    b, l, d = x.shape
    xt = x.reshape(b * l, d)
    f32 = jnp.float32
    group_logits = jnp.dot(xt, w_group, preferred_element_type=f32) + b_group.astype(f32)
    group_prob = jax.nn.softmax(group_logits, axis=-1)
    p_group, g_sel = lax.top_k(group_prob, 1)
    exp_logits = (jnp.dot(xt, w_router, preferred_element_type=f32) + b_router.astype(f32)
                  ).reshape(-1, N_GROUPS, EXPERTS_PER_GROUP)
    in_group = jnp.einsum('ng,nge->ne', jax.nn.one_hot(g_sel[:, 0], N_GROUPS, dtype=f32), exp_logits)
    top_logit, top_idx = lax.top_k(in_group, TOP_K)
    gate = jax.nn.softmax(top_logit, axis=-1) * p_group
    expert_id = g_sel * EXPERTS_PER_GROUP + top_idx
    combine = jnp.einsum('nke,nk->ne', jax.nn.one_hot(expert_id, N_EXPERTS, dtype=f32), gate).astype(x.dtype)
    out = jnp.zeros_like(xt)
    for e in range(N_EXPERTS):
        hidden = jax.nn.silu(xt @ w1[e]) * (xt @ w3[e])
        out = out + combine[:, e:e + 1] * (hidden @ w2[e])
    return out.reshape(b, l, d)


def encoder_trunk(x, ln_in_g, ln_in_b, w_in, b_in, short_w, short_b, na_rpb, flt_w1, flt_b1, flt_w2, flt_b2,
                  flt_w3, flt_b3, flt_w4, flt_freq, flt_bias, w_branch_attn, w_branch_hyena, w_out,
                  ln1_g, ln1_b, w_group, b_group, w_router, b_router, w1, w3, w2, ln2_g, ln2_b):
    h = layer_norm(x, ln_in_g, ln_in_b)
    for i in range(DEPTH):
        mix = token_mixer(h, w_in[i], b_in[i], short_w[i], short_b[i], na_rpb[i], flt_w1[i], flt_b1[i],
                          flt_w2[i], flt_b2[i], flt_w3[i], flt_b3[i], flt_w4[i], flt_freq[i], flt_bias[i],
                          w_branch_attn[i], w_branch_hyena[i], w_out[i])
        h = layer_norm(ALPHA * h + mix, ln1_g[i], ln1_b[i])
        ffn = hierarchical_moe(h, w_group[i], b_group[i], w_router[i], b_router[i], w1[i], w3[i], w2[i])
        h = layer_norm(ALPHA * h + ffn, ln2_g[i], ln2_b[i])
    return h


def setup_inputs(seed: int = 0) -> dict:
    key = jax.random.key(seed)
    keys = iter(jax.random.split(key, 40))

    def nrm(shape, scale):
        return jax.random.normal(next(keys), shape, jnp.float32) * scale

    return {
        'x_prompt': nrm((BATCH, SEQ, D_MODEL), 1.0),
        'x_sample': nrm((DEC_BATCH, DEC_SEQ, D_MODEL), 1.0),
        'ln_in_g': 1.0 + nrm((D_MODEL,), 0.02),
        'ln_in_b': nrm((D_MODEL,), 0.02),
        'w_in': nrm((DEPTH, D_MODEL, D_IN), D_MODEL ** -0.5),
        'b_in': nrm((DEPTH, D_IN), 0.02),
        'short_w': nrm((DEPTH, SHORT_CONV, 3 * D_HYENA), SHORT_CONV ** -0.5),
        'short_b': nrm((DEPTH, 3 * D_HYENA), 0.02),
        'na_rpb': nrm((DEPTH, NA_HEADS, 2 * NA_WIN_ROWS - 1, 2 * NA_WIN_COLS - 1), 0.02),
        'flt_w1': nrm((DEPTH, FILTER_EMB, FILTER_HIDDEN), FILTER_EMB ** -0.5),
        'flt_b1': nrm((DEPTH, FILTER_HIDDEN), 0.1),
        'flt_w2': nrm((DEPTH, FILTER_HIDDEN, FILTER_HIDDEN), FILTER_HIDDEN ** -0.5),
        'flt_b2': nrm((DEPTH, FILTER_HIDDEN), 0.1),
        'flt_w3': nrm((DEPTH, FILTER_HIDDEN, FILTER_HIDDEN), FILTER_HIDDEN ** -0.5),
        'flt_b3': nrm((DEPTH, FILTER_HIDDEN), 0.1),
        'flt_w4': nrm((DEPTH, FILTER_HIDDEN, 2 * D_HYENA), FILTER_OUT_STD),
        'flt_freq': 1.0 + nrm((DEPTH, FILTER_HIDDEN), 0.02),
        'flt_bias': nrm((DEPTH, D_HYENA), 1.0),
        'w_branch_attn': nrm((DEPTH, D_ATTN, D_MODEL), D_ATTN ** -0.5),
        'w_branch_hyena': nrm((DEPTH, D_HYENA, D_MODEL), D_HYENA ** -0.5),
        'w_out': nrm((DEPTH, D_MODEL, D_MODEL), BETA * D_MODEL ** -0.5),
        'ln1_g': 1.0 + nrm((DEPTH, D_MODEL), 0.02),
        'ln1_b': nrm((DEPTH, D_MODEL), 0.02),
        'w_group': nrm((DEPTH, D_MODEL, N_GROUPS), D_MODEL ** -0.5),
        'b_group': nrm((DEPTH, N_GROUPS), 0.01),
        'w_router': nrm((DEPTH, D_MODEL, N_EXPERTS), D_MODEL ** -0.5),
        'b_router': nrm((DEPTH, N_EXPERTS), 0.01),
        'w1': nrm((DEPTH, N_EXPERTS, D_MODEL, D_EXPERT), D_MODEL ** -0.5),
        'w3': nrm((DEPTH, N_EXPERTS, D_MODEL, D_EXPERT), D_MODEL ** -0.5),
        'w2': nrm((DEPTH, N_EXPERTS, D_EXPERT, D_MODEL), BETA * D_EXPERT ** -0.5),
        'ln2_g': 1.0 + nrm((DEPTH, D_MODEL), 0.02),
        'ln2_b': nrm((DEPTH, D_MODEL), 0.02),
    }


def reference(x_prompt, x_sample, ln_in_g, ln_in_b, w_in, b_in, short_w, short_b, na_rpb, flt_w1, flt_b1,
              flt_w2, flt_b2, flt_w3, flt_b3, flt_w4, flt_freq, flt_bias, w_branch_attn, w_branch_hyena, w_out,
              ln1_g, ln1_b, w_group, b_group, w_router, b_router, w1, w3, w2, ln2_g, ln2_b):
    weights = (ln_in_g, ln_in_b, w_in, b_in, short_w, short_b, na_rpb, flt_w1, flt_b1, flt_w2, flt_b2,
               flt_w3, flt_b3, flt_w4, flt_freq, flt_bias, w_branch_attn, w_branch_hyena, w_out,
               ln1_g, ln1_b, w_group, b_group, w_router, b_router, w1, w3, w2, ln2_g, ln2_b)
    y_prompt = encoder_trunk(x_prompt, *weights)
    y_sample = encoder_trunk(x_sample, *weights)
    return (y_prompt, y_sample)
```

```python
import functools
import math

import numpy as np
import jax
import jax.numpy as jnp
from jax import lax
from jax.experimental import pallas as pl
from jax.experimental.pallas import tpu as pltpu

F32 = jnp.float32
BF16 = jnp.bfloat16

D_MODEL = 1024
DEPTH = 2
GRID_W = 64
NA_HEADS = 8
NA_HEAD_DIM = 64
D_ATTN = NA_HEADS * NA_HEAD_DIM
NA_WIN_ROWS = 8
NA_WIN_COLS = 16
D_HYENA = D_MODEL // 2
FILTER_EMB = 33
FILTER_HIDDEN = 64
DECAY_FAST = 0.3
DECAY_SLOW = 1.5
DECAY_TARGET = 1e-2
N_GROUPS = 4
EXPERTS_PER_GROUP = 4
N_EXPERTS = N_GROUPS * EXPERTS_PER_GROUP
D_EXPERT = 512
LN_EPS = 1e-5
ALPHA = (2.0 * DEPTH) ** 0.25
D_QKVH = 3 * D_ATTN + 3 * D_HYENA
D_GATES = 2 * D_MODEL

LANES = 128
VMEM_LIMIT = 56 * 1024 * 1024
NEG_BIG = -1e30


def _params(*sem):
    return pltpu.CompilerParams(dimension_semantics=sem, vmem_limit_bytes=VMEM_LIMIT)


def _layer_norm(x, g, b):
    mu = jnp.mean(x, axis=-1, keepdims=True)
    xc = x - mu
    var = jnp.mean(xc * xc, axis=-1, keepdims=True)
    return xc * lax.rsqrt(var + LN_EPS) * g + b


def _row_tile(n, want):
    t = min(n, want)
    assert n % t == 0, (n, t)
    return t


def _inproj_kernel(apply_ln, x_ref, g_ref, b_ref, w_ref, bias_ref, *out_refs):
    x = x_ref[...]
    if apply_ln:
        h_ref, q_ref, k_ref, v_ref, hy_ref = out_refs
        x = _layer_norm(x, g_ref[...], b_ref[...])
        h_ref[...] = x
    else:
        q_ref, k_ref, v_ref, hy_ref = out_refs
    xb = x.astype(BF16)

    def proj(lo, hi):
        return jnp.dot(xb, w_ref[:, lo:hi], preferred_element_type=F32) + bias_ref[:, lo:hi]

    q_ref[...] = (proj(0, D_ATTN) * (NA_HEAD_DIM ** -0.5)).astype(BF16)
    k_ref[...] = proj(D_ATTN, 2 * D_ATTN).astype(BF16)
    v_ref[...] = proj(2 * D_ATTN, 3 * D_ATTN).astype(BF16)
    hy_ref[...] = proj(3 * D_ATTN, D_QKVH)


def _inproj(x, ln_g, ln_b, w, bias, apply_ln):
    n = x.shape[0]
    tm = _row_tile(n, 512)
    row = lambda i: (i, 0)
    fixed = lambda i: (0, 0)
    out_shape = [jax.ShapeDtypeStruct((n, D_ATTN), BF16)] * 3 + [jax.ShapeDtypeStruct((n, 3 * D_HYENA), F32)]
    out_specs = [pl.BlockSpec((tm, D_ATTN), row)] * 3 + [pl.BlockSpec((tm, 3 * D_HYENA), row)]
    if apply_ln:
        out_shape = [jax.ShapeDtypeStruct((n, D_MODEL), F32)] + out_shape
        out_specs = [pl.BlockSpec((tm, D_MODEL), row)] + out_specs
    return pl.pallas_call(
        functools.partial(_inproj_kernel, apply_ln),
        grid=(n // tm,),
        in_specs=[pl.BlockSpec((tm, D_MODEL), row), pl.BlockSpec((1, D_MODEL), fixed),
                  pl.BlockSpec((1, D_MODEL), fixed), pl.BlockSpec((D_MODEL, D_QKVH), fixed),
                  pl.BlockSpec((1, D_QKVH), fixed)],
        out_specs=out_specs,
        out_shape=out_shape,
        compiler_params=_params("parallel"),
        name="inproj",
    )(x, ln_g, ln_b, w, bias)


def _na_bias_table(rpb):
    cls = np.arange(NA_WIN_ROWS)
    m = np.arange(NA_WIN_ROWS)
    row_off = m[None, :] - cls[:, None] + (NA_WIN_ROWS - 1)
    t = np.arange(GRID_W)
    kc = np.arange(GRID_W)
    col_start = np.clip(t - NA_WIN_COLS // 2, 0, GRID_W - NA_WIN_COLS)
    valid = (kc[None, :] >= col_start[:, None]) & (kc[None, :] < col_start[:, None] + NA_WIN_COLS)
    col_off = np.clip(kc[None, :] - t[:, None], 1 - NA_WIN_COLS, NA_WIN_COLS - 1) + (NA_WIN_COLS - 1)
    g = rpb.astype(F32)[:, row_off[:, :, None, None], col_off[None, None, :, :]]
    g = jnp.where(jnp.asarray(valid)[None, None, None, :, :], g, NEG_BIG)
    g = jnp.transpose(g, (1, 0, 3, 2, 4))
    return g.reshape(NA_WIN_ROWS, NA_HEADS * GRID_W, NA_WIN_ROWS * GRID_W)


def _head_mask():
    r = np.arange(NA_HEADS * GRID_W)[:, None] // GRID_W
    c = np.arange(D_ATTN)[None, :] // NA_HEAD_DIM
    return (r == c).astype(np.float32)


def _na_kernel(rows, q_ref, k_ref, v_ref, bias_ref, hm_ref, o_ref):
    r = pl.program_id(1)
    first = jnp.clip(r - NA_WIN_ROWS // 2, 0, rows - NA_WIN_ROWS)
    start = pl.multiple_of(first * GRID_W, GRID_W)
    nkeys = NA_WIN_ROWS * GRID_W
    kw = k_ref[pl.ds(start, nkeys), :]
    vw = v_ref[pl.ds(start, nkeys), :]
    hm = hm_ref[...]
    q = q_ref[...]
    qbd = jnp.concatenate([q] * NA_HEADS, axis=0) * hm
    s = lax.dot_general(qbd, kw, (((1,), (1,)), ((), ())), preferred_element_type=F32)
    s = s + bias_ref[...]
    s = s - jnp.max(s, axis=-1, keepdims=True)
    e = jnp.exp(s)
    p = (e * (1.0 / jnp.sum(e, axis=-1, keepdims=True))).astype(BF16)
    pv = jnp.dot(p, vw, preferred_element_type=F32) * hm.astype(F32)
    out = pv[0:GRID_W]
    for h in range(1, NA_HEADS):
        out = out + pv[h * GRID_W:(h + 1) * GRID_W]
    o_ref[...] = out.astype(BF16)


def _neighbourhood_attention(q, k, v, bias_table, head_mask):
    b, l, _ = q.shape
    rows = l // GRID_W
    assert rows >= NA_WIN_ROWS and l % GRID_W == 0
    nkeys = NA_WIN_ROWS * GRID_W
    nq = NA_HEADS * GRID_W
    cls = lambda bi, r: (r - jnp.clip(r - NA_WIN_ROWS // 2, 0, rows - NA_WIN_ROWS), 0, 0)
    return pl.pallas_call(
        functools.partial(_na_kernel, rows),
        grid=(b, rows),
        in_specs=[pl.BlockSpec((None, GRID_W, D_ATTN), lambda bi, r: (bi, r, 0)),
                  pl.BlockSpec((None, l, D_ATTN), lambda bi, r: (bi, 0, 0)),
                  pl.BlockSpec((None, l, D_ATTN), lambda bi, r: (bi, 0, 0)),
                  pl.BlockSpec((None, nq, nkeys), cls),
                  pl.BlockSpec((nq, D_ATTN), lambda bi, r: (0, 0))],
        out_specs=pl.BlockSpec((None, GRID_W, D_ATTN), lambda bi, r: (bi, r, 0)),
        out_shape=jax.ShapeDtypeStruct((b, l, D_ATTN), BF16),
        compiler_params=_params("parallel", "arbitrary"),
        name="nattn",
    )(q, k, v, bias_table, head_mask)


def _short_conv_kernel(nt, cur_ref, prev_ref, next_ref, sw_ref, sb_ref, u_ref, x0_ref):
    i = pl.program_id(1)
    cur = cur_ref[...]
    tl = cur.shape[0]
    row = lax.broadcasted_iota(jnp.int32, cur.shape, 0)
    prev_row = jnp.where(i > 0, prev_ref[7:8, :], 0.0)
    next_row = jnp.where(i < nt - 1, next_ref[0:1, :], 0.0)
    up = jnp.where(row == 0, prev_row, pltpu.roll(cur, 1, axis=0))
    dn = jnp.where(row == tl - 1, next_row, pltpu.roll(cur, tl - 1, axis=0))
    c = sw_ref[0:1, :] * up + sw_ref[1:2, :] * cur + sw_ref[2:3, :] * dn + sb_ref[...]
    x0 = c[:, 0:D_HYENA]
    x1 = c[:, D_HYENA:2 * D_HYENA]
    vv = c[:, 2 * D_HYENA:3 * D_HYENA]
    u_ref[...] = (vv * x1).astype(BF16)
    x0_ref[...] = x0.astype(BF16)


def _short_conv(hy, short_w, short_b):
    b, l, c3 = hy.shape
    tl = _row_tile(l, 512)
    nt = l // tl
    sub = tl // 8
    cur = lambda bi, i: (bi, i, 0)
    prev = lambda bi, i: (bi, jnp.maximum(i * sub - 1, 0), 0)
    nxt = lambda bi, i: (bi, jnp.minimum((i + 1) * sub, l // 8 - 1), 0)
    fixed = lambda bi, i: (0, 0)
    return pl.pallas_call(
        functools.partial(_short_conv_kernel, nt),
        grid=(b, nt),
        in_specs=[pl.BlockSpec((None, tl, c3), cur), pl.BlockSpec((None, 8, c3), prev),
                  pl.BlockSpec((None, 8, c3), nxt), pl.BlockSpec((3, c3), fixed),
                  pl.BlockSpec((1, c3), fixed)],
        out_specs=[pl.BlockSpec((None, tl, D_HYENA), cur)] * 2,
        out_shape=[jax.ShapeDtypeStruct((b, l, D_HYENA), BF16)] * 2,
        compiler_params=_params("parallel", "parallel"),
        name="short_conv",
    )(hy, hy, hy, short_w, short_b)


def _filter_features(l):
    t = jnp.linspace(0.0, 1.0, l, dtype=F32)[:, None]
    bands = (FILTER_EMB - 1) // 2
    f = jnp.linspace(1e-4, bands - 1, bands, dtype=F32)[None, :]
    ang = f * (2.0 * math.pi / l) * jnp.arange(l, dtype=F32)[:, None]
    z = jnp.concatenate([t, jnp.cos(ang), -jnp.sin(ang)], axis=-1)
    r = np.arange(2 * l)
    src = np.where(r < l, r, np.where(r == l, 0, 2 * l - r))
    z = z[src]
    return jnp.pad(z, ((0, 0), (0, LANES - FILTER_EMB)))


def _decay_rates():
    max_decay = math.log(DECAY_TARGET) / DECAY_FAST
    min_decay = math.log(DECAY_TARGET) / DECAY_SLOW
    return jnp.abs(jnp.linspace(min_decay, max_decay, D_HYENA, dtype=F32))[None, :]


def _filter_kernel(l, z_ref, w1_ref, b1_ref, w2_ref, b2_ref, w3_ref, b3_ref, w4_ref, freq_ref,
                   rate_ref, o_ref):
    hp = lax.Precision.HIGHEST
    z = z_ref[...]
    freq = freq_ref[...]
    h = jnp.sin(freq * (jnp.dot(z, w1_ref[...], precision=hp, preferred_element_type=F32) + b1_ref[...]))
    h = jnp.sin(freq * (jnp.dot(h, w2_ref[...], precision=hp, preferred_element_type=F32) + b2_ref[...]))
    h = jnp.sin(freq * (jnp.dot(h, w3_ref[...], precision=hp, preferred_element_type=F32) + b3_ref[...]))
    h = jnp.dot(h, w4_ref[...], precision=hp, preferred_element_type=F32)
    tr = z.shape[0]
    r = pl.program_id(0) * tr + lax.broadcasted_iota(jnp.int32, (tr, D_HYENA), 0)
    decay = jnp.exp(-z[:, 0:1] * rate_ref[...])
    tap = jnp.where(r < l, h[:, 0:D_HYENA], jnp.where(r > l, h[:, D_HYENA:], 0.0))
    o_ref[...] = (tap * decay * (0.5 / l)).astype(BF16)


def _filter_taps(l, z, w1, b1, w2, b2, w3, b3, w4, freq, rates):
    tr = _row_tile(2 * l, 512)
    fixed = lambda i: (0, 0)
    full = lambda a: pl.BlockSpec(a.shape, fixed)
    args = (w1, b1, w2, b2, w3, b3, w4, freq, rates)
    return pl.pallas_call(
        functools.partial(_filter_kernel, l),
        grid=(2 * l // tr,),
        in_specs=[pl.BlockSpec((tr, LANES), lambda i: (i, 0))] + [full(a) for a in args],
        out_specs=pl.BlockSpec((tr, D_HYENA), lambda i: (i, 0)),
        out_shape=jax.ShapeDtypeStruct((2 * l, D_HYENA), BF16),
        compiler_params=_params("parallel"),
        name="filter_taps",
    )(z, *args)


class _DftPlan:
    def __init__(self, l):
        n = 2 * l
        lg = n.bit_length() - 1
        assert 1 << lg == n
        self.n1 = n1 = 1 << (lg // 2)
        self.n2 = n2 = n // n1
        self.half = half = n1 // 2
        assert half % 16 == 0
        k1 = np.arange(n1)
        f1 = np.exp(-2j * np.pi * np.outer(k1, np.arange(n1)) / n1)
        f1h = f1[:, :half]
        self.a1_data = np.block([[f1h.real, -f1h.imag], [f1h.imag, f1h.real]]).astype(np.float32)
        self.a1_filt = np.concatenate([f1.real, f1.imag], axis=0).astype(np.float32)
        e = np.exp(2j * np.pi * np.outer(np.arange(half), k1) / n1)
        self.a3 = np.block([[e.real, -e.imag], [e.imag, e.real]]).astype(np.float32)
        tw = np.exp(-2j * np.pi * np.outer(k1, np.arange(n2)) / n)
        self.tw_r = tw.real.astype(np.float32)
        self.tw_i = tw.imag.astype(np.float32)
        f2 = np.exp(-2j * np.pi * np.outer(np.arange(n2), np.arange(n2)) / n2)
        self.f2_r = f2.real.astype(np.float32)
        self.f2_i = f2.imag.astype(np.float32)


def _colblock(cols):
    return min(cols, 8192)


def _dft_rows_kernel(a_ref, x_ref, o_ref):
    x = x_ref[...]
    x = x.reshape(x.shape[0] * x.shape[1], x.shape[2])
    o_ref[...] = jnp.dot(a_ref[...], x, preferred_element_type=F32).astype(BF16)


def _dft_rows(a, x):
    p, _, half, cols = x.shape
    m = a.shape[0]
    cb = _colblock(cols)
    return pl.pallas_call(
        _dft_rows_kernel,
        grid=(p, cols // cb),
        in_specs=[pl.BlockSpec(a.shape, lambda pi, j: (0, 0)),
                  pl.BlockSpec((None, 2, half, cb), lambda pi, j: (pi, 0, 0, j))],
        out_specs=pl.BlockSpec((None, m, cb), lambda pi, j: (pi, 0, j)),
        out_shape=jax.ShapeDtypeStruct((p, m, cols), BF16),
        compiler_params=_params("parallel", "parallel"),
        name="dft_rows",
    )(a, x)


def _twiddled_dft(f2r, f2i, twr, twi):
    gr = f2r * twr - f2i * twi
    gi = f2r * twi + f2i * twr
    return gr, gi


def _filter_spectrum_kernel(kb, z_ref, f2r_ref, f2i_ref, twr_ref, twi_ref, o_ref):
    f2r = f2r_ref[...]
    f2i = f2i_ref[...]

    def body(k, carry):
        gr, gi = _twiddled_dft(f2r, f2i, twr_ref[k], twi_ref[k])
        a_re = jnp.concatenate([gr, gi], axis=0).astype(BF16)
        a_im = jnp.concatenate([-gi, gr], axis=0).astype(BF16)
        y = (jnp.dot(a_re, z_ref[0, k], preferred_element_type=F32)
             + jnp.dot(a_im, z_ref[1, k], preferred_element_type=F32))
        n2 = f2r.shape[0]
        o_ref[0, k] = y[0:n2]
        o_ref[1, k] = y[n2:]
        return carry

    lax.fori_loop(0, kb, body, 0)


def _filter_spectrum(plan, z):
    _, n1, n2, c = z.shape
    kb = min(n1, 8)
    blk = lambda i: (0, i, 0, 0)
    fixed = lambda i: (0, 0)
    return pl.pallas_call(
        functools.partial(_filter_spectrum_kernel, kb),
        grid=(n1 // kb,),
        in_specs=[pl.BlockSpec((2, kb, n2, c), blk), pl.BlockSpec((n2, n2), fixed),
                  pl.BlockSpec((n2, n2), fixed), pl.BlockSpec((kb, 1, n2), lambda i: (i, 0, 0)),
                  pl.BlockSpec((kb, 1, n2), lambda i: (i, 0, 0))],
        out_specs=pl.BlockSpec((2, kb, n2, c), blk),
        out_shape=jax.ShapeDtypeStruct((2, n1, n2, c), F32),
        compiler_params=_params("parallel"),
        name="filter_spectrum",
    )(z, plan.f2_r, plan.f2_i, plan.tw_r[:, None, :], plan.tw_i[:, None, :])


def _spectral_kernel(kb, z_ref, kf_ref, f2r_ref, f2i_ref, twr_ref, twi_ref, twrc_ref, twic_ref, o_ref):
    f2r = f2r_ref[...]
    f2i = f2i_ref[...]
    n2 = f2r.shape[0]

    def body(k, carry):
        gr, gi = _twiddled_dft(f2r, f2i, twr_ref[k], twi_ref[k])
        a_re = jnp.concatenate([gr, gi], axis=0).astype(BF16)
        a_im = jnp.concatenate([-gi, gr], axis=0).astype(BF16)
        y = (jnp.dot(a_re, z_ref[0, k], preferred_element_type=F32)
             + jnp.dot(a_im, z_ref[1, k], preferred_element_type=F32))
        yr, yi = y[0:n2], y[n2:]
        kr, ki = kf_ref[0, k], kf_ref[1, k]
        wr = (yr * kr - yi * ki).astype(BF16)
        wi = (yr * ki + yi * kr).astype(BF16)
        cr, ci = _twiddled_dft(f2r, f2i, twrc_ref[k], twic_ref[k])
        b_re = jnp.concatenate([cr, -ci], axis=0).astype(BF16)
        b_im = jnp.concatenate([ci, cr], axis=0).astype(BF16)
        v = (jnp.dot(b_re, wr, preferred_element_type=F32)
             + jnp.dot(b_im, wi, preferred_element_type=F32))
        o_ref[0, k] = v[0:n2].astype(BF16)
        o_ref[1, k] = v[n2:].astype(BF16)
        return carry

    lax.fori_loop(0, kb, body, 0)


def _spectral_multiply(plan, z, kf):
    p, _, n1, n2, c = z.shape
    kb = min(n1, 8)
    zblk = lambda i, pi: (pi, 0, i, 0, 0)
    fixed = lambda i, pi: (0, 0)
    trow = pl.BlockSpec((kb, 1, n2), lambda i, pi: (i, 0, 0))
    tcol = pl.BlockSpec((kb, n2, 1), lambda i, pi: (i, 0, 0))
    return pl.pallas_call(
        functools.partial(_spectral_kernel, kb),
        grid=(n1 // kb, p),
        in_specs=[pl.BlockSpec((None, 2, kb, n2, c), zblk),
                  pl.BlockSpec((2, kb, n2, c), lambda i, pi: (0, i, 0, 0)),
                  pl.BlockSpec((n2, n2), fixed), pl.BlockSpec((n2, n2), fixed),
                  trow, trow, tcol, tcol],
        out_specs=pl.BlockSpec((None, 2, kb, n2, c), zblk),
        out_shape=jax.ShapeDtypeStruct(z.shape, BF16),
        compiler_params=_params("parallel", "arbitrary"),
        name="spectral_multiply",
    )(z, kf, plan.f2_r, plan.f2_i, plan.tw_r[:, None, :], plan.tw_i[:, None, :],
      plan.tw_r[:, :, None], plan.tw_i[:, :, None])


def _idft_rows_kernel(a_ref, v_ref, u_ref, x0_ref, bias_ref, o_ref):
    y = jnp.dot(a_ref[...], v_ref[...], preferred_element_type=F32)
    u = u_ref[...].astype(F32)
    y = y.reshape(u.shape) + u * bias_ref[...]
    o_ref[...] = (y * x0_ref[...].astype(F32)).astype(BF16)


def _idft_rows(a3, v, u, x0, bias_cols):
    p, _, half, cols = u.shape
    cb = _colblock(cols)
    ublk = pl.BlockSpec((None, 2, half, cb), lambda pi, j: (pi, 0, 0, j))
    return pl.pallas_call(
        _idft_rows_kernel,
        grid=(p, cols // cb),
        in_specs=[pl.BlockSpec(a3.shape, lambda pi, j: (0, 0)),
                  pl.BlockSpec((None, v.shape[1], cb), lambda pi, j: (pi, 0, j)),
                  ublk, ublk, pl.BlockSpec((1, cb), lambda pi, j: (0, 0))],
        out_specs=ublk,
        out_shape=jax.ShapeDtypeStruct(u.shape, BF16),
        compiler_params=_params("parallel", "parallel"),
        name="idft_rows",
    )(a3, v, u, x0, bias_cols)


def _hyena(hy, plan, kf, short_w, short_b, flt_bias):
    b, l, _ = hy.shape
    assert b % 2 == 0
    u, x0 = _short_conv(hy, short_w, short_b)
    cols = plan.n2 * D_HYENA
    pairs = lambda a: a.reshape(b // 2, 2, plan.half, cols)
    z = _dft_rows(jnp.asarray(plan.a1_data, BF16), pairs(u))
    z = z.reshape(b // 2, 2, plan.n1, plan.n2, D_HYENA)
    v = _spectral_multiply(plan, z, kf).reshape(b // 2, 2 * plan.n1, cols)
    bias_cols = jnp.tile(flt_bias.astype(F32)[None, :], (1, _colblock(cols) // D_HYENA))
    o = _idft_rows(jnp.asarray(plan.a3, BF16), v, pairs(u), pairs(x0), bias_cols)
    return o.reshape(b * l, D_HYENA)


def _hyena_filter_spectrum(l, plan, z_feat, w1, b1, w2, b2, w3, b3, w4, freq):
    w1p = jnp.pad(w1, ((0, LANES - FILTER_EMB), (0, 0)))
    taps = _filter_taps(l, z_feat, w1p, b1[None, :], w2, b2[None, :], w3, b3[None, :], w4,
                        freq[None, :], _decay_rates())
    cols = plan.n2 * D_HYENA
    z = _dft_rows(jnp.asarray(plan.a1_filt, BF16), taps.reshape(1, 2, plan.half, cols))
    return _filter_spectrum(plan, z.reshape(2, plan.n1, plan.n2, D_HYENA))


def _sigmoid(x):
    return 1.0 / (1.0 + jnp.exp(-x))


def _merge_kernel(h_ref, oa_ref, oh_ref, wg_ref, bg_ref, wa_ref, wh_ref, wo_ref, g_ref, b_ref, o_ref):
    h = h_ref[...]
    hb = h.astype(BF16)
    ga = jnp.dot(hb, wg_ref[:, 0:D_MODEL], preferred_element_type=F32) + bg_ref[:, 0:D_MODEL]
    gh = jnp.dot(hb, wg_ref[:, D_MODEL:], preferred_element_type=F32) + bg_ref[:, D_MODEL:]
    merged = (_sigmoid(ga) * jnp.dot(oa_ref[...], wa_ref[...], preferred_element_type=F32)
              + _sigmoid(gh) * jnp.dot(oh_ref[...], wh_ref[...], preferred_element_type=F32))
    mix = jnp.dot(merged.astype(BF16), wo_ref[...], preferred_element_type=F32)
    o_ref[...] = _layer_norm(ALPHA * h + mix, g_ref[...], b_ref[...])


def _merge(h, o_attn, o_hyena, wg, bg, wa, wh, wo, ln_g, ln_b):
    n = h.shape[0]
    tm = _row_tile(n, 512)
    row = lambda i: (i, 0)
    fixed = lambda i: (0, 0)
    full = lambda a: pl.BlockSpec(a.shape, fixed)
    return pl.pallas_call(
        _merge_kernel,
        grid=(n // tm,),
        in_specs=[pl.BlockSpec((tm, D_MODEL), row), pl.BlockSpec((tm, D_ATTN), row),
                  pl.BlockSpec((tm, D_HYENA), row)] + [full(a) for a in (wg, bg, wa, wh, wo, ln_g, ln_b)],
        out_specs=pl.BlockSpec((tm, D_MODEL), row),
        out_shape=jax.ShapeDtypeStruct((n, D_MODEL), F32),
        compiler_params=_params("parallel"),
        name="merge",
    )(h, o_attn, o_hyena, wg, bg, wa, wh, wo, ln_g, ln_b)


def _router_kernel(x_ref, w_ref, b_ref, o_ref):
    logits = jnp.dot(x_ref[...], w_ref[...], precision=lax.Precision.HIGHEST,
                     preferred_element_type=F32) + b_ref[...]
    lane_i = lax.broadcasted_iota(jnp.int32, logits.shape, 1)
    lane = lane_i.astype(F32)
    group_of_lane = jnp.right_shift(lane_i, 2).astype(F32)
    neg = -jnp.inf

    def first_argmax(vals, vmax):
        return jnp.min(jnp.where(vals == vmax, lane, float(LANES)), axis=-1, keepdims=True)

    gl = jnp.where((lane_i >= N_EXPERTS) & (lane_i < N_EXPERTS + N_GROUPS), logits, neg)
    gmax = jnp.max(gl, axis=-1, keepdims=True)
    p_group = 1.0 / jnp.sum(jnp.exp(gl - gmax), axis=-1, keepdims=True)
    g_sel = first_argmax(gl, gmax) - float(N_EXPERTS)
    el = jnp.where((lane_i < N_EXPERTS) & (group_of_lane == g_sel), logits, neg)
    m1 = jnp.max(el, axis=-1, keepdims=True)
    i1 = first_argmax(el, m1)
    el2 = jnp.where(lane == i1, neg, el)
    m2 = jnp.max(el2, axis=-1, keepdims=True)
    i2 = first_argmax(el2, m2)
    e2 = jnp.exp(m2 - m1)
    denom = 1.0 / (1.0 + e2)
    gate1 = denom * p_group
    gate2 = e2 * denom * p_group
    o_ref[...] = jnp.where(lane == i1, gate1, 0.0) + jnp.where(lane == i2, gate2, 0.0)


def _router(x, w, b):
    n = x.shape[0]
    tm = _row_tile(n, 1024)
    return pl.pallas_call(
        _router_kernel,
        grid=(n // tm,),
        in_specs=[pl.BlockSpec((tm, D_MODEL), lambda i: (i, 0)),
                  pl.BlockSpec((D_MODEL, LANES), lambda i: (0, 0)),
                  pl.BlockSpec((1, LANES), lambda i: (0, 0))],
        out_specs=pl.BlockSpec((tm, LANES), lambda i: (i, 0)),
        out_shape=jax.ShapeDtypeStruct((n, LANES), F32),
        compiler_params=_params("parallel"),
        name="router",
    )(x, w, b)


def _experts_kernel(x_ref, c_ref, w1_ref, w3_ref, w2_ref, g_ref, b_ref, o_ref, xb_ref, acc_ref):
    e = pl.program_id(1)

    @pl.when(e == 0)
    def _():
        xb_ref[...] = x_ref[...].astype(BF16)
        acc_ref[...] = jnp.zeros_like(acc_ref)

    xb = xb_ref[...]
    a = jnp.dot(xb, w1_ref[...], preferred_element_type=F32)
    g = jnp.dot(xb, w3_ref[...], preferred_element_type=F32)
    hidden = (a * _sigmoid(a) * g).astype(BF16)
    comb = c_ref[...]
    lane = lax.broadcasted_iota(jnp.int32, comb.shape, 1)
    ce = jnp.sum(jnp.where(lane == e, comb, 0.0), axis=-1, keepdims=True)
    acc_ref[...] += ce * jnp.dot(hidden, w2_ref[...], preferred_element_type=F32)

    @pl.when(e == N_EXPERTS - 1)
    def _():
        o_ref[...] = _layer_norm(ALPHA * x_ref[...] + acc_ref[...], g_ref[...], b_ref[...])


def _experts(x, combine, w1, w3, w2, ln_g, ln_b):
    n = x.shape[0]
    tm = _row_tile(n, 1024)
    row = lambda i, e: (i, 0)
    fixed = lambda i, e: (0, 0)
    return pl.pallas_call(
        _experts_kernel,
        grid=(n // tm, N_EXPERTS),
        in_specs=[pl.BlockSpec((tm, D_MODEL), row), pl.BlockSpec((tm, LANES), row),
                  pl.BlockSpec((None, D_MODEL, D_EXPERT), lambda i, e: (e, 0, 0)),
                  pl.BlockSpec((None, D_MODEL, D_EXPERT), lambda i, e: (e, 0, 0)),
                  pl.BlockSpec((None, D_EXPERT, D_MODEL), lambda i, e: (e, 0, 0)),
                  pl.BlockSpec((1, D_MODEL), fixed), pl.BlockSpec((1, D_MODEL), fixed)],
        out_specs=pl.BlockSpec((tm, D_MODEL), row),
        out_shape=jax.ShapeDtypeStruct((n, D_MODEL), F32),
        scratch_shapes=[pltpu.VMEM((tm, D_MODEL), BF16), pltpu.VMEM((tm, D_MODEL), F32)],
        compiler_params=_params("parallel", "arbitrary"),
        name="experts",
    )(x, combine, w1, w3, w2, ln_g, ln_b)


def _prepare_layer(i, w_in, b_in, short_w, short_b, na_rpb, w_branch_attn, w_branch_hyena, w_out,
                   ln1_g, ln1_b, w_group, b_group, w_router, b_router, w1, w3, w2, ln2_g, ln2_b, flt_bias):
    row = lambda a: a[None, :].astype(F32)
    w_route = jnp.zeros((D_MODEL, LANES), F32)
    w_route = w_route.at[:, 0:N_EXPERTS].set(w_router[i]).at[:, N_EXPERTS:N_EXPERTS + N_GROUPS].set(w_group[i])
    b_route = jnp.zeros((1, LANES), F32)
    b_route = b_route.at[0, 0:N_EXPERTS].set(b_router[i]).at[0, N_EXPERTS:N_EXPERTS + N_GROUPS].set(b_group[i])
    return dict(
        w_qkvh=w_in[i][:, :D_QKVH].astype(BF16), b_qkvh=row(b_in[i][:D_QKVH]),
        w_gates=w_in[i][:, D_QKVH:].astype(BF16), b_gates=row(b_in[i][D_QKVH:]),
        short_w=short_w[i], short_b=row(short_b[i]), flt_bias=flt_bias[i],
        na_bias=_na_bias_table(na_rpb[i]),
        wa=w_branch_attn[i].astype(BF16), wh=w_branch_hyena[i].astype(BF16), wo=w_out[i].astype(BF16),
        ln1_g=row(ln1_g[i]), ln1_b=row(ln1_b[i]), ln2_g=row(ln2_g[i]), ln2_b=row(ln2_b[i]),
        w_route=w_route, b_route=b_route,
        w1=w1[i].astype(BF16), w3=w3[i].astype(BF16), w2=w2[i].astype(BF16),
    )


def _trunk(x, ln_in_g, ln_in_b, layers, filt):
    b, l, d = x.shape
    n = b * l
    plan = _DftPlan(l)
    z_feat = _filter_features(l)
    head_mask = jnp.asarray(_head_mask(), BF16)
    h = x.reshape(n, d)
    for i, lay in enumerate(layers):
        outs = _inproj(h, ln_in_g[None, :], ln_in_b[None, :], lay["w_qkvh"], lay["b_qkvh"], apply_ln=(i == 0))
        if i == 0:
            h, q, k, v, hy = outs
        else:
            q, k, v, hy = outs
        seq = lambda a: a.reshape(b, l, a.shape[-1])
        o_attn = _neighbourhood_attention(seq(q), seq(k), seq(v), lay["na_bias"], head_mask).reshape(n, D_ATTN)
        kf = _hyena_filter_spectrum(l, plan, z_feat, *[f[i] for f in filt])
        o_hyena = _hyena(seq(hy), plan, kf, lay["short_w"], lay["short_b"], lay["flt_bias"])
        h = _merge(h, o_attn, o_hyena, lay["w_gates"], lay["b_gates"], lay["wa"], lay["wh"], lay["wo"],
                   lay["ln1_g"], lay["ln1_b"])
        combine = _router(h, lay["w_route"], lay["b_route"])
        h = _experts(h, combine, lay["w1"], lay["w3"], lay["w2"], lay["ln2_g"], lay["ln2_b"])
    return h.reshape(b, l, d)


def kernel(x_prompt, x_sample, ln_in_g, ln_in_b, w_in, b_in, short_w, short_b, na_rpb, flt_w1, flt_b1, flt_w2, flt_b2, flt_w3, flt_b3, flt_w4, flt_freq, flt_bias, w_branch_attn, w_branch_hyena, w_out, ln1_g, ln1_b, w_group, b_group, w_router, b_router, w1, w3, w2, ln2_g, ln2_b):
    layers = [_prepare_layer(i, w_in, b_in, short_w, short_b, na_rpb, w_branch_attn, w_branch_hyena, w_out,
                             ln1_g, ln1_b, w_group, b_group, w_router, b_router, w1, w3, w2, ln2_g, ln2_b,
                             flt_bias) for i in range(DEPTH)]
    filt = (flt_w1, flt_b1, flt_w2, flt_b2, flt_w3, flt_b3, flt_w4, flt_freq)
    y_prompt = _trunk(x_prompt, ln_in_g, ln_in_b, layers, filt)
    y_sample = _trunk(x_sample, ln_in_g, ln_in_b, layers, filt)
    return (y_prompt, y_sample)
```

```python
import functools
import math

import numpy as np
import jax
import jax.numpy as jnp
from jax import lax
from jax.experimental import pallas as pl
from jax.experimental.pallas import tpu as pltpu

F32 = jnp.float32
BF16 = jnp.bfloat16

D_MODEL = 1024
DEPTH = 2
GRID_W = 64
NA_HEADS = 8
NA_HEAD_DIM = 64
D_ATTN = NA_HEADS * NA_HEAD_DIM
NA_WIN_ROWS = 8
NA_WIN_COLS = 16
D_HYENA = D_MODEL // 2
FILTER_EMB = 33
FILTER_HIDDEN = 64
DECAY_FAST = 0.3
DECAY_SLOW = 1.5
DECAY_TARGET = 1e-2
N_GROUPS = 4
EXPERTS_PER_GROUP = 4
N_EXPERTS = N_GROUPS * EXPERTS_PER_GROUP
D_EXPERT = 512
LN_EPS = 1e-5
ALPHA = (2.0 * DEPTH) ** 0.25
D_QKVH = 3 * D_ATTN + 3 * D_HYENA
D_GATES = 2 * D_MODEL

LANES = 128
VMEM_LIMIT = 56 * 1024 * 1024
NEG_BIG = -1e30


def _params(*sem):
    return pltpu.CompilerParams(dimension_semantics=sem, vmem_limit_bytes=VMEM_LIMIT)


def _layer_norm(x, g, b):
    mu = jnp.mean(x, axis=-1, keepdims=True)
    xc = x - mu
    var = jnp.mean(xc * xc, axis=-1, keepdims=True)
    return xc * lax.rsqrt(var + LN_EPS) * g + b


def _row_tile(n, want):
    t = min(n, want)
    assert n % t == 0, (n, t)
    return t


def _inproj_kernel(apply_ln, x_ref, g_ref, b_ref, w_ref, bias_ref, *out_refs):
    x = x_ref[...]
    if apply_ln:
        h_ref, q_ref, k_ref, v_ref, hy_ref = out_refs
        x = _layer_norm(x, g_ref[...], b_ref[...])
        h_ref[...] = x
    else:
        q_ref, k_ref, v_ref, hy_ref = out_refs
    xb = x.astype(BF16)

    def proj(lo, hi):
        return jnp.dot(xb, w_ref[:, lo:hi], preferred_element_type=F32) + bias_ref[:, lo:hi]

    q_ref[...] = (proj(0, D_ATTN) * (NA_HEAD_DIM ** -0.5)).astype(BF16)
    k_ref[...] = proj(D_ATTN, 2 * D_ATTN).astype(BF16)
    v_ref[...] = proj(2 * D_ATTN, 3 * D_ATTN).astype(BF16)
    hy_ref[...] = proj(3 * D_ATTN, D_QKVH)


def _inproj(x, ln_g, ln_b, w, bias, apply_ln):
    n = x.shape[0]
    tm = _row_tile(n, 512)
    row = lambda i: (i, 0)
    fixed = lambda i: (0, 0)
    out_shape = [jax.ShapeDtypeStruct((n, D_ATTN), BF16)] * 3 + [jax.ShapeDtypeStruct((n, 3 * D_HYENA), F32)]
    out_specs = [pl.BlockSpec((tm, D_ATTN), row)] * 3 + [pl.BlockSpec((tm, 3 * D_HYENA), row)]
    if apply_ln:
        out_shape = [jax.ShapeDtypeStruct((n, D_MODEL), F32)] + out_shape
        out_specs = [pl.BlockSpec((tm, D_MODEL), row)] + out_specs
    return pl.pallas_call(
        functools.partial(_inproj_kernel, apply_ln),
        grid=(n // tm,),
        in_specs=[pl.BlockSpec((tm, D_MODEL), row), pl.BlockSpec((1, D_MODEL), fixed),
                  pl.BlockSpec((1, D_MODEL), fixed), pl.BlockSpec((D_MODEL, D_QKVH), fixed),
                  pl.BlockSpec((1, D_QKVH), fixed)],
        out_specs=out_specs,
        out_shape=out_shape,
        compiler_params=_params("parallel"),
        name="inproj",
    )(x, ln_g, ln_b, w, bias)


def _na_bias_table(rpb):
    t = np.arange(GRID_W)
    kc = np.arange(GRID_W)
    col_start = np.clip(t - NA_WIN_COLS // 2, 0, GRID_W - NA_WIN_COLS)
    valid = (kc[None, :] >= col_start[:, None]) & (kc[None, :] < col_start[:, None] + NA_WIN_COLS)
    col_off = np.clip(kc[None, :] - t[:, None], 1 - NA_WIN_COLS, NA_WIN_COLS - 1) + (NA_WIN_COLS - 1)
    n_off = 2 * NA_WIN_COLS - 1
    pick = (col_off[:, :, None] == np.arange(n_off)[None, None, :]).astype(np.float32)
    g = jnp.einsum('hjs,tks->htjk', rpb.astype(F32), pick, precision=lax.Precision.HIGHEST)
    g = jnp.where(jnp.asarray(valid)[None, :, None, :], g, NEG_BIG)
    top = NA_WIN_ROWS - 1
    g = jnp.stack([g[:, :, top - c:top - c + NA_WIN_ROWS, :] for c in range(NA_WIN_ROWS)], axis=0)
    return g.reshape(NA_WIN_ROWS, NA_HEADS * GRID_W, NA_WIN_ROWS * GRID_W)


def _head_mask():
    r = np.arange(NA_HEADS * GRID_W)[:, None] // GRID_W
    c = np.arange(D_ATTN)[None, :] // NA_HEAD_DIM
    return (r == c).astype(np.float32)


def _na_kernel(rows, q_ref, k_ref, v_ref, bias_ref, hm_ref, o_ref):
    r = pl.program_id(1)
    first = jnp.clip(r - NA_WIN_ROWS // 2, 0, rows - NA_WIN_ROWS)
    start = pl.multiple_of(first * GRID_W, GRID_W)
    nkeys = NA_WIN_ROWS * GRID_W
    kw = k_ref[pl.ds(start, nkeys), :]
    vw = v_ref[pl.ds(start, nkeys), :]
    hm = hm_ref[...]
    q = q_ref[...]
    qbd = jnp.concatenate([q] * NA_HEADS, axis=0) * hm
    s = lax.dot_general(qbd, kw, (((1,), (1,)), ((), ())), preferred_element_type=F32)
    s = s + bias_ref[...]
    s = s - jnp.max(s, axis=-1, keepdims=True)
    e = jnp.exp(s)
    p = (e * (1.0 / jnp.sum(e, axis=-1, keepdims=True))).astype(BF16)
    pv = jnp.dot(p, vw, preferred_element_type=F32) * hm.astype(F32)
    out = pv[0:GRID_W]
    for h in range(1, NA_HEADS):
        out = out + pv[h * GRID_W:(h + 1) * GRID_W]
    o_ref[...] = out.astype(BF16)


def _neighbourhood_attention(q, k, v, bias_table, head_mask):
    b, l, _ = q.shape
    rows = l // GRID_W
    assert rows >= NA_WIN_ROWS and l % GRID_W == 0
    nkeys = NA_WIN_ROWS * GRID_W
    nq = NA_HEADS * GRID_W
    cls = lambda bi, r: (r - jnp.clip(r - NA_WIN_ROWS // 2, 0, rows - NA_WIN_ROWS), 0, 0)
    return pl.pallas_call(
        functools.partial(_na_kernel, rows),
        grid=(b, rows),
        in_specs=[pl.BlockSpec((None, GRID_W, D_ATTN), lambda bi, r: (bi, r, 0)),
                  pl.BlockSpec((None, l, D_ATTN), lambda bi, r: (bi, 0, 0)),
                  pl.BlockSpec((None, l, D_ATTN), lambda bi, r: (bi, 0, 0)),
                  pl.BlockSpec((None, nq, nkeys), cls),
                  pl.BlockSpec((nq, D_ATTN), lambda bi, r: (0, 0))],
        out_specs=pl.BlockSpec((None, GRID_W, D_ATTN), lambda bi, r: (bi, r, 0)),
        out_shape=jax.ShapeDtypeStruct((b, l, D_ATTN), BF16),
        compiler_params=_params("parallel", "arbitrary"),
        name="nattn",
    )(q, k, v, bias_table, head_mask)


def _short_conv_kernel(nt, cur_ref, prev_ref, next_ref, sw_ref, sb_ref, u_ref, x0_ref):
    i = pl.program_id(1)
    cur = cur_ref[...]
    tl = cur.shape[0]
    row = lax.broadcasted_iota(jnp.int32, cur.shape, 0)
    prev_row = jnp.where(i > 0, prev_ref[7:8, :], 0.0)
    next_row = jnp.where(i < nt - 1, next_ref[0:1, :], 0.0)
    up = jnp.where(row == 0, prev_row, pltpu.roll(cur, 1, axis=0))
    dn = jnp.where(row == tl - 1, next_row, pltpu.roll(cur, tl - 1, axis=0))
    c = sw_ref[0:1, :] * up + sw_ref[1:2, :] * cur + sw_ref[2:3, :] * dn + sb_ref[...]
    x0 = c[:, 0:D_HYENA]
    x1 = c[:, D_HYENA:2 * D_HYENA]
    vv = c[:, 2 * D_HYENA:3 * D_HYENA]
    u_ref[...] = (vv * x1).astype(BF16)
    x0_ref[...] = x0.astype(BF16)


def _short_conv(hy, short_w, short_b):
    b, l, c3 = hy.shape
    tl = _row_tile(l, 512)
    nt = l // tl
    sub = tl // 8
    cur = lambda bi, i: (bi, i, 0)
    prev = lambda bi, i: (bi, jnp.maximum(i * sub - 1, 0), 0)
    nxt = lambda bi, i: (bi, jnp.minimum((i + 1) * sub, l // 8 - 1), 0)
    fixed = lambda bi, i: (0, 0)
    return pl.pallas_call(
        functools.partial(_short_conv_kernel, nt),
        grid=(b, nt),
        in_specs=[pl.BlockSpec((None, tl, c3), cur), pl.BlockSpec((None, 8, c3), prev),
                  pl.BlockSpec((None, 8, c3), nxt), pl.BlockSpec((3, c3), fixed),
                  pl.BlockSpec((1, c3), fixed)],
        out_specs=[pl.BlockSpec((None, tl, D_HYENA), cur)] * 2,
        out_shape=[jax.ShapeDtypeStruct((b, l, D_HYENA), BF16)] * 2,
        compiler_params=_params("parallel", "parallel"),
        name="short_conv",
    )(hy, hy, hy, short_w, short_b)


def _filter_features(l):
    t = jnp.linspace(0.0, 1.0, l, dtype=F32)[:, None]
    bands = (FILTER_EMB - 1) // 2
    f = jnp.linspace(1e-4, bands - 1, bands, dtype=F32)[None, :]
    ang = f * (2.0 * math.pi / l) * jnp.arange(l, dtype=F32)[:, None]
    z = jnp.concatenate([t, jnp.cos(ang), -jnp.sin(ang)], axis=-1)
    r = np.arange(2 * l)
    src = np.where(r < l, r, np.where(r == l, 0, 2 * l - r))
    z = z[src]
    return jnp.pad(z, ((0, 0), (0, LANES - FILTER_EMB)))


def _decay_rates():
    max_decay = math.log(DECAY_TARGET) / DECAY_FAST
    min_decay = math.log(DECAY_TARGET) / DECAY_SLOW
    return jnp.abs(jnp.linspace(min_decay, max_decay, D_HYENA, dtype=F32))[None, :]


def _filter_kernel(l, z_ref, w1_ref, b1_ref, w2_ref, b2_ref, w3_ref, b3_ref, w4_ref, freq_ref,
                   rate_ref, o_ref):
    hp = lax.Precision.HIGHEST
    z = z_ref[...]
    freq = freq_ref[...]
    h = jnp.sin(freq * (jnp.dot(z, w1_ref[...], precision=hp, preferred_element_type=F32) + b1_ref[...]))
    h = jnp.sin(freq * (jnp.dot(h, w2_ref[...], precision=hp, preferred_element_type=F32) + b2_ref[...]))
    h = jnp.sin(freq * (jnp.dot(h, w3_ref[...], precision=hp, preferred_element_type=F32) + b3_ref[...]))
    h = jnp.dot(h, w4_ref[...], precision=hp, preferred_element_type=F32)
    tr = z.shape[0]
    r = pl.program_id(0) * tr + lax.broadcasted_iota(jnp.int32, (tr, D_HYENA), 0)
    decay = jnp.exp(-z[:, 0:1] * rate_ref[...])
    tap = jnp.where(r < l, h[:, 0:D_HYENA], jnp.where(r > l, h[:, D_HYENA:], 0.0))
    o_ref[...] = (tap * decay * (0.5 / l)).astype(BF16)


def _filter_taps(l, z, w1, b1, w2, b2, w3, b3, w4, freq, rates):
    tr = _row_tile(2 * l, 512)
    fixed = lambda i: (0, 0)
    full = lambda a: pl.BlockSpec(a.shape, fixed)
    args = (w1, b1, w2, b2, w3, b3, w4, freq, rates)
    return pl.pallas_call(
        functools.partial(_filter_kernel, l),
        grid=(2 * l // tr,),
        in_specs=[pl.BlockSpec((tr, LANES), lambda i: (i, 0))] + [full(a) for a in args],
        out_specs=pl.BlockSpec((tr, D_HYENA), lambda i: (i, 0)),
        out_shape=jax.ShapeDtypeStruct((2 * l, D_HYENA), BF16),
        compiler_params=_params("parallel"),
        name="filter_taps",
    )(z, *args)


class _DftPlan:
    def __init__(self, l):
        n = 2 * l
        lg = n.bit_length() - 1
        assert 1 << lg == n
        self.n1 = n1 = 1 << (lg // 2)
        self.n2 = n2 = n // n1
        self.half = half = n1 // 2
        assert half % 16 == 0
        k1 = np.arange(n1)
        f1 = np.exp(-2j * np.pi * np.outer(k1, np.arange(n1)) / n1)
        f1h = f1[:, :half]
        self.a1_data = np.block([[f1h.real, -f1h.imag], [f1h.imag, f1h.real]]).astype(np.float32)
        self.a1_filt = np.concatenate([f1.real, f1.imag], axis=0).astype(np.float32)
        e = np.exp(2j * np.pi * np.outer(np.arange(half), k1) / n1)
        self.a3 = np.block([[e.real, -e.imag], [e.imag, e.real]]).astype(np.float32)
        tw = np.exp(-2j * np.pi * np.outer(k1, np.arange(n2)) / n)
        self.tw_r = tw.real.astype(np.float32)
        self.tw_i = tw.imag.astype(np.float32)
        f2 = np.exp(-2j * np.pi * np.outer(np.arange(n2), np.arange(n2)) / n2)
        self.f2_r = f2.real.astype(np.float32)
        self.f2_i = f2.imag.astype(np.float32)


def _colblock(cols):
    return min(cols, 8192)


def _dft_rows_kernel(a_ref, x_ref, o_ref):
    x = x_ref[...]
    x = x.reshape(x.shape[0] * x.shape[1], x.shape[2])
    o_ref[...] = jnp.dot(a_ref[...], x, preferred_element_type=F32).astype(BF16)


def _dft_rows(a, x):
    p, _, half, cols = x.shape
    m = a.shape[0]
    cb = _colblock(cols)
    return pl.pallas_call(
        _dft_rows_kernel,
        grid=(p, cols // cb),
        in_specs=[pl.BlockSpec(a.shape, lambda pi, j: (0, 0)),
                  pl.BlockSpec((None, 2, half, cb), lambda pi, j: (pi, 0, 0, j))],
        out_specs=pl.BlockSpec((None, m, cb), lambda pi, j: (pi, 0, j)),
        out_shape=jax.ShapeDtypeStruct((p, m, cols), BF16),
        compiler_params=_params("parallel", "parallel"),
        name="dft_rows",
    )(a, x)


def _twiddled_dft(f2r, f2i, twr, twi):
    gr = f2r * twr - f2i * twi
    gi = f2r * twi + f2i * twr
    return gr, gi


def _filter_spectrum_kernel(kb, z_ref, f2r_ref, f2i_ref, twr_ref, twi_ref, o_ref):
    f2r = f2r_ref[...]
    f2i = f2i_ref[...]

    def body(k, carry):
        gr, gi = _twiddled_dft(f2r, f2i, twr_ref[k], twi_ref[k])
        a_re = jnp.concatenate([gr, gi], axis=0).astype(BF16)
        a_im = jnp.concatenate([-gi, gr], axis=0).astype(BF16)
        y = (jnp.dot(a_re, z_ref[0, k], preferred_element_type=F32)
             + jnp.dot(a_im, z_ref[1, k], preferred_element_type=F32))
        n2 = f2r.shape[0]
        o_ref[0, k] = y[0:n2]
        o_ref[1, k] = y[n2:]
        return carry

    lax.fori_loop(0, kb, body, 0)


def _filter_spectrum(plan, z):
    _, n1, n2, c = z.shape
    kb = min(n1, 8)
    blk = lambda i: (0, i, 0, 0)
    fixed = lambda i: (0, 0)
    return pl.pallas_call(
        functools.partial(_filter_spectrum_kernel, kb),
        grid=(n1 // kb,),
        in_specs=[pl.BlockSpec((2, kb, n2, c), blk), pl.BlockSpec((n2, n2), fixed),
                  pl.BlockSpec((n2, n2), fixed), pl.BlockSpec((kb, 1, n2), lambda i: (i, 0, 0)),
                  pl.BlockSpec((kb, 1, n2), lambda i: (i, 0, 0))],
        out_specs=pl.BlockSpec((2, kb, n2, c), blk),
        out_shape=jax.ShapeDtypeStruct((2, n1, n2, c), F32),
        compiler_params=_params("parallel"),
        name="filter_spectrum",
    )(z, plan.f2_r, plan.f2_i, plan.tw_r[:, None, :], plan.tw_i[:, None, :])


def _spectral_kernel(kb, z_ref, kf_ref, f2r_ref, f2i_ref, twr_ref, twi_ref, twrc_ref, twic_ref, o_ref):
    f2r = f2r_ref[...]
    f2i = f2i_ref[...]
    n2 = f2r.shape[0]

    def body(k, carry):
        gr, gi = _twiddled_dft(f2r, f2i, twr_ref[k], twi_ref[k])
        a_re = jnp.concatenate([gr, gi], axis=0).astype(BF16)
        a_im = jnp.concatenate([-gi, gr], axis=0).astype(BF16)
        y = (jnp.dot(a_re, z_ref[0, k], preferred_element_type=F32)
             + jnp.dot(a_im, z_ref[1, k], preferred_element_type=F32))
        yr, yi = y[0:n2], y[n2:]
        kr, ki = kf_ref[0, k], kf_ref[1, k]
        wr = (yr * kr - yi * ki).astype(BF16)
        wi = (yr * ki + yi * kr).astype(BF16)
        cr, ci = _twiddled_dft(f2r, f2i, twrc_ref[k], twic_ref[k])
        b_re = jnp.concatenate([cr, -ci], axis=0).astype(BF16)
        b_im = jnp.concatenate([ci, cr], axis=0).astype(BF16)
        v = (jnp.dot(b_re, wr, preferred_element_type=F32)
             + jnp.dot(b_im, wi, preferred_element_type=F32))
        o_ref[0, k] = v[0:n2].astype(BF16)
        o_ref[1, k] = v[n2:].astype(BF16)
        return carry

    lax.fori_loop(0, kb, body, 0)


def _spectral_multiply(plan, z, kf):
    p, _, n1, n2, c = z.shape
    kb = min(n1, 8)
    zblk = lambda i, pi: (pi, 0, i, 0, 0)
    fixed = lambda i, pi: (0, 0)
    trow = pl.BlockSpec((kb, 1, n2), lambda i, pi: (i, 0, 0))
    tcol = pl.BlockSpec((kb, n2, 1), lambda i, pi: (i, 0, 0))
    return pl.pallas_call(
        functools.partial(_spectral_kernel, kb),
        grid=(n1 // kb, p),
        in_specs=[pl.BlockSpec((None, 2, kb, n2, c), zblk),
                  pl.BlockSpec((2, kb, n2, c), lambda i, pi: (0, i, 0, 0)),
                  pl.BlockSpec((n2, n2), fixed), pl.BlockSpec((n2, n2), fixed),
                  trow, trow, tcol, tcol],
        out_specs=pl.BlockSpec((None, 2, kb, n2, c), zblk),
        out_shape=jax.ShapeDtypeStruct(z.shape, BF16),
        compiler_params=_params("parallel", "arbitrary"),
        name="spectral_multiply",
    )(z, kf, plan.f2_r, plan.f2_i, plan.tw_r[:, None, :], plan.tw_i[:, None, :],
      plan.tw_r[:, :, None], plan.tw_i[:, :, None])


def _idft_rows_kernel(a_ref, v_ref, u_ref, x0_ref, bias_ref, o_ref):
    y = jnp.dot(a_ref[...], v_ref[...], preferred_element_type=F32)
    u = u_ref[...].astype(F32)
    y = y.reshape(u.shape) + u * bias_ref[...]
    o_ref[...] = (y * x0_ref[...].astype(F32)).astype(BF16)


def _idft_rows(a3, v, u, x0, bias_cols):
    p, _, half, cols = u.shape
    cb = _colblock(cols)
    ublk = pl.BlockSpec((None, 2, half, cb), lambda pi, j: (pi, 0, 0, j))
    return pl.pallas_call(
        _idft_rows_kernel,
        grid=(p, cols // cb),
        in_specs=[pl.BlockSpec(a3.shape, lambda pi, j: (0, 0)),
                  pl.BlockSpec((None, v.shape[1], cb), lambda pi, j: (pi, 0, j)),
                  ublk, ublk, pl.BlockSpec((1, cb), lambda pi, j: (0, 0))],
        out_specs=ublk,
        out_shape=jax.ShapeDtypeStruct(u.shape, BF16),
        compiler_params=_params("parallel", "parallel"),
        name="idft_rows",
    )(a3, v, u, x0, bias_cols)


def _hyena(hy, plan, kf, short_w, short_b, flt_bias):
    b, l, _ = hy.shape
    assert b % 2 == 0
    u, x0 = _short_conv(hy, short_w, short_b)
    cols = plan.n2 * D_HYENA
    pairs = lambda a: a.reshape(b // 2, 2, plan.half, cols)
    z = _dft_rows(jnp.asarray(plan.a1_data, BF16), pairs(u))
    z = z.reshape(b // 2, 2, plan.n1, plan.n2, D_HYENA)
    v = _spectral_multiply(plan, z, kf).reshape(b // 2, 2 * plan.n1, cols)
    bias_cols = jnp.tile(flt_bias.astype(F32)[None, :], (1, _colblock(cols) // D_HYENA))
    o = _idft_rows(jnp.asarray(plan.a3, BF16), v, pairs(u), pairs(x0), bias_cols)
    return o.reshape(b * l, D_HYENA)


def _hyena_filter_spectrum(l, plan, z_feat, w1, b1, w2, b2, w3, b3, w4, freq):
    w1p = jnp.pad(w1, ((0, LANES - FILTER_EMB), (0, 0)))
    taps = _filter_taps(l, z_feat, w1p, b1[None, :], w2, b2[None, :], w3, b3[None, :], w4,
                        freq[None, :], _decay_rates())
    cols = plan.n2 * D_HYENA
    z = _dft_rows(jnp.asarray(plan.a1_filt, BF16), taps.reshape(1, 2, plan.half, cols))
    return _filter_spectrum(plan, z.reshape(2, plan.n1, plan.n2, D_HYENA))


def _sigmoid(x):
    return 1.0 / (1.0 + jnp.exp(-x))


def _merge_kernel(h_ref, oa_ref, oh_ref, wg_ref, bg_ref, wa_ref, wh_ref, wo_ref, g_ref, b_ref, o_ref):
    h = h_ref[...]
    hb = h.astype(BF16)
    ga = jnp.dot(hb, wg_ref[:, 0:D_MODEL], preferred_element_type=F32) + bg_ref[:, 0:D_MODEL]
    gh = jnp.dot(hb, wg_ref[:, D_MODEL:], preferred_element_type=F32) + bg_ref[:, D_MODEL:]
    merged = (_sigmoid(ga) * jnp.dot(oa_ref[...], wa_ref[...], preferred_element_type=F32)
              + _sigmoid(gh) * jnp.dot(oh_ref[...], wh_ref[...], preferred_element_type=F32))
    mix = jnp.dot(merged.astype(BF16), wo_ref[...], preferred_element_type=F32)
    o_ref[...] = _layer_norm(ALPHA * h + mix, g_ref[...], b_ref[...])


def _merge(h, o_attn, o_hyena, wg, bg, wa, wh, wo, ln_g, ln_b):
    n = h.shape[0]
    tm = _row_tile(n, 512)
    row = lambda i: (i, 0)
    fixed = lambda i: (0, 0)
    full = lambda a: pl.BlockSpec(a.shape, fixed)
    return pl.pallas_call(
        _merge_kernel,
        grid=(n // tm,),
        in_specs=[pl.BlockSpec((tm, D_MODEL), row), pl.BlockSpec((tm, D_ATTN), row),
                  pl.BlockSpec((tm, D_HYENA), row)] + [full(a) for a in (wg, bg, wa, wh, wo, ln_g, ln_b)],
        out_specs=pl.BlockSpec((tm, D_MODEL), row),
        out_shape=jax.ShapeDtypeStruct((n, D_MODEL), F32),
        compiler_params=_params("parallel"),
        name="merge",
    )(h, o_attn, o_hyena, wg, bg, wa, wh, wo, ln_g, ln_b)


def _router_kernel(x_ref, w_ref, b_ref, o_ref):
    logits = jnp.dot(x_ref[...], w_ref[...], precision=lax.Precision.HIGHEST,
                     preferred_element_type=F32) + b_ref[...]
    lane_i = lax.broadcasted_iota(jnp.int32, logits.shape, 1)
    lane = lane_i.astype(F32)
    group_of_lane = jnp.right_shift(lane_i, 2).astype(F32)
    neg = -jnp.inf

    def first_argmax(vals, vmax):
        return jnp.min(jnp.where(vals == vmax, lane, float(LANES)), axis=-1, keepdims=True)

    gl = jnp.where((lane_i >= N_EXPERTS) & (lane_i < N_EXPERTS + N_GROUPS), logits, neg)
    gmax = jnp.max(gl, axis=-1, keepdims=True)
    p_group = 1.0 / jnp.sum(jnp.exp(gl - gmax), axis=-1, keepdims=True)
    g_sel = first_argmax(gl, gmax) - float(N_EXPERTS)
    el = jnp.where((lane_i < N_EXPERTS) & (group_of_lane == g_sel), logits, neg)
    m1 = jnp.max(el, axis=-1, keepdims=True)
    i1 = first_argmax(el, m1)
    el2 = jnp.where(lane == i1, neg, el)
    m2 = jnp.max(el2, axis=-1, keepdims=True)
    i2 = first_argmax(el2, m2)
    e2 = jnp.exp(m2 - m1)
    denom = 1.0 / (1.0 + e2)
    gate1 = denom * p_group
    gate2 = e2 * denom * p_group
    o_ref[...] = jnp.where(lane == i1, gate1, 0.0) + jnp.where(lane == i2, gate2, 0.0)


def _router(x, w, b):
    n = x.shape[0]
    tm = _row_tile(n, 1024)
    return pl.pallas_call(
        _router_kernel,
        grid=(n // tm,),
        in_specs=[pl.BlockSpec((tm, D_MODEL), lambda i: (i, 0)),
                  pl.BlockSpec((D_MODEL, LANES), lambda i: (0, 0)),
                  pl.BlockSpec((1, LANES), lambda i: (0, 0))],
        out_specs=pl.BlockSpec((tm, LANES), lambda i: (i, 0)),
        out_shape=jax.ShapeDtypeStruct((n, LANES), F32),
        compiler_params=_params("parallel"),
        name="router",
    )(x, w, b)


def _experts_kernel(x_ref, c_ref, w1_ref, w3_ref, w2_ref, g_ref, b_ref, o_ref, xb_ref, acc_ref):
    e = pl.program_id(1)

    @pl.when(e == 0)
    def _():
        xb_ref[...] = x_ref[...].astype(BF16)
        acc_ref[...] = jnp.zeros_like(acc_ref)

    xb = xb_ref[...]
    a = jnp.dot(xb, w1_ref[...], preferred_element_type=F32)
    g = jnp.dot(xb, w3_ref[...], preferred_element_type=F32)
    hidden = (a * _sigmoid(a) * g).astype(BF16)
    comb = c_ref[...]
    lane = lax.broadcasted_iota(jnp.int32, comb.shape, 1)
    ce = jnp.sum(jnp.where(lane == e, comb, 0.0), axis=-1, keepdims=True)
    acc_ref[...] += ce * jnp.dot(hidden, w2_ref[...], preferred_element_type=F32)

    @pl.when(e == N_EXPERTS - 1)
    def _():
        o_ref[...] = _layer_norm(ALPHA * x_ref[...] + acc_ref[...], g_ref[...], b_ref[...])


def _experts(x, combine, w1, w3, w2, ln_g, ln_b):
    n = x.shape[0]
    tm = _row_tile(n, 1024)
    row = lambda i, e: (i, 0)
    fixed = lambda i, e: (0, 0)
    return pl.pallas_call(
        _experts_kernel,
        grid=(n // tm, N_EXPERTS),
        in_specs=[pl.BlockSpec((tm, D_MODEL), row), pl.BlockSpec((tm, LANES), row),
                  pl.BlockSpec((None, D_MODEL, D_EXPERT), lambda i, e: (e, 0, 0)),
                  pl.BlockSpec((None, D_MODEL, D_EXPERT), lambda i, e: (e, 0, 0)),
                  pl.BlockSpec((None, D_EXPERT, D_MODEL), lambda i, e: (e, 0, 0)),
                  pl.BlockSpec((1, D_MODEL), fixed), pl.BlockSpec((1, D_MODEL), fixed)],
        out_specs=pl.BlockSpec((tm, D_MODEL), row),
        out_shape=jax.ShapeDtypeStruct((n, D_MODEL), F32),
        scratch_shapes=[pltpu.VMEM((tm, D_MODEL), BF16), pltpu.VMEM((tm, D_MODEL), F32)],
        compiler_params=_params("parallel", "arbitrary"),
        name="experts",
    )(x, combine, w1, w3, w2, ln_g, ln_b)


def _prepare_layer(i, w_in, b_in, short_w, short_b, na_rpb, w_branch_attn, w_branch_hyena, w_out,
                   ln1_g, ln1_b, w_group, b_group, w_router, b_router, w1, w3, w2, ln2_g, ln2_b, flt_bias):
    row = lambda a: a[None, :].astype(F32)
    w_route = jnp.zeros((D_MODEL, LANES), F32)
    w_route = w_route.at[:, 0:N_EXPERTS].set(w_router[i]).at[:, N_EXPERTS:N_EXPERTS + N_GROUPS].set(w_group[i])
    b_route = jnp.zeros((1, LANES), F32)
    b_route = b_route.at[0, 0:N_EXPERTS].set(b_router[i]).at[0, N_EXPERTS:N_EXPERTS + N_GROUPS].set(b_group[i])
    return dict(
        w_qkvh=w_in[i][:, :D_QKVH].astype(BF16), b_qkvh=row(b_in[i][:D_QKVH]),
        w_gates=w_in[i][:, D_QKVH:].astype(BF16), b_gates=row(b_in[i][D_QKVH:]),
        short_w=short_w[i], short_b=row(short_b[i]), flt_bias=flt_bias[i],
        na_bias=_na_bias_table(na_rpb[i]),
        wa=w_branch_attn[i].astype(BF16), wh=w_branch_hyena[i].astype(BF16), wo=w_out[i].astype(BF16),
        ln1_g=row(ln1_g[i]), ln1_b=row(ln1_b[i]), ln2_g=row(ln2_g[i]), ln2_b=row(ln2_b[i]),
        w_route=w_route, b_route=b_route,
        w1=w1[i].astype(BF16), w3=w3[i].astype(BF16), w2=w2[i].astype(BF16),
    )


def _trunk(x, ln_in_g, ln_in_b, layers, filt):
    b, l, d = x.shape
    n = b * l
    plan = _DftPlan(l)
    z_feat = _filter_features(l)
    head_mask = jnp.asarray(_head_mask(), BF16)
    h = x.reshape(n, d)
    for i, lay in enumerate(layers):
        outs = _inproj(h, ln_in_g[None, :], ln_in_b[None, :], lay["w_qkvh"], lay["b_qkvh"], apply_ln=(i == 0))
        if i == 0:
            h, q, k, v, hy = outs
        else:
            q, k, v, hy = outs
        seq = lambda a: a.reshape(b, l, a.shape[-1])
        o_attn = _neighbourhood_attention(seq(q), seq(k), seq(v), lay["na_bias"], head_mask).reshape(n, D_ATTN)
        kf = _hyena_filter_spectrum(l, plan, z_feat, *[f[i] for f in filt])
        o_hyena = _hyena(seq(hy), plan, kf, lay["short_w"], lay["short_b"], lay["flt_bias"])
        h = _merge(h, o_attn, o_hyena, lay["w_gates"], lay["b_gates"], lay["wa"], lay["wh"], lay["wo"],
                   lay["ln1_g"], lay["ln1_b"])
        combine = _router(h, lay["w_route"], lay["b_route"])
        h = _experts(h, combine, lay["w1"], lay["w3"], lay["w2"], lay["ln2_g"], lay["ln2_b"])
    return h.reshape(b, l, d)


def kernel(x_prompt, x_sample, ln_in_g, ln_in_b, w_in, b_in, short_w, short_b, na_rpb, flt_w1, flt_b1, flt_w2, flt_b2, flt_w3, flt_b3, flt_w4, flt_freq, flt_bias, w_branch_attn, w_branch_hyena, w_out, ln1_g, ln1_b, w_group, b_group, w_router, b_router, w1, w3, w2, ln2_g, ln2_b):
    layers = [_prepare_layer(i, w_in, b_in, short_w, short_b, na_rpb, w_branch_attn, w_branch_hyena, w_out,
                             ln1_g, ln1_b, w_group, b_group, w_router, b_router, w1, w3, w2, ln2_g, ln2_b,
                             flt_bias) for i in range(DEPTH)]
    filt = (flt_w1, flt_b1, flt_w2, flt_b2, flt_w3, flt_b3, flt_w4, flt_freq)
    y_prompt = _trunk(x_prompt, ln_in_g, ln_in_b, layers, filt)
    y_sample = _trunk(x_sample, ln_in_g, ln_in_b, layers, filt)
    return (y_prompt, y_sample)
```

```python
import functools
import math

import numpy as np
import jax
import jax.numpy as jnp
from jax import lax
from jax.experimental import pallas as pl
from jax.experimental.pallas import tpu as pltpu

F32 = jnp.float32
BF16 = jnp.bfloat16

D_MODEL = 1024
DEPTH = 2
GRID_W = 64
NA_HEADS = 8
NA_HEAD_DIM = 64
D_ATTN = NA_HEADS * NA_HEAD_DIM
NA_WIN_ROWS = 8
NA_WIN_COLS = 16
D_HYENA = D_MODEL // 2
FILTER_EMB = 33
FILTER_HIDDEN = 64
DECAY_FAST = 0.3
DECAY_SLOW = 1.5
DECAY_TARGET = 1e-2
N_GROUPS = 4
EXPERTS_PER_GROUP = 4
N_EXPERTS = N_GROUPS * EXPERTS_PER_GROUP
D_EXPERT = 512
LN_EPS = 1e-5
ALPHA = (2.0 * DEPTH) ** 0.25
D_QKVH = 3 * D_ATTN + 3 * D_HYENA
D_GATES = 2 * D_MODEL

LANES = 128
VMEM_LIMIT = 56 * 1024 * 1024
NEG_BIG = -1e30
LOG2E = math.log2(math.e)


def _params(*sem):
    return pltpu.CompilerParams(dimension_semantics=sem, vmem_limit_bytes=VMEM_LIMIT)


def _layer_norm(x, g, b):
    mu = jnp.mean(x, axis=-1, keepdims=True)
    xc = x - mu
    var = jnp.mean(xc * xc, axis=-1, keepdims=True)
    return xc * lax.rsqrt(var + LN_EPS) * g + b


def _row_tile(n, want):
    t = min(n, want)
    assert n % t == 0, (n, t)
    return t


def _inproj_kernel(apply_ln, x_ref, g_ref, b_ref, w_ref, bias_ref, *out_refs):
    x = x_ref[...]
    if apply_ln:
        h_ref, q_ref, k_ref, v_ref, hy_ref = out_refs
        x = _layer_norm(x, g_ref[...], b_ref[...])
        h_ref[...] = x
    else:
        q_ref, k_ref, v_ref, hy_ref = out_refs
    xb = x.astype(BF16)

    def proj(lo, hi):
        return jnp.dot(xb, w_ref[:, lo:hi], preferred_element_type=F32) + bias_ref[:, lo:hi]

    q_ref[...] = (proj(0, D_ATTN) * (NA_HEAD_DIM ** -0.5 * LOG2E)).astype(BF16)
    k_ref[...] = proj(D_ATTN, 2 * D_ATTN).astype(BF16)
    v_ref[...] = proj(2 * D_ATTN, 3 * D_ATTN).astype(BF16)
    hy_ref[...] = proj(3 * D_ATTN, D_QKVH)


def _inproj(x, ln_g, ln_b, w, bias, apply_ln):
    n = x.shape[0]
    tm = _row_tile(n, 512)
    row = lambda i: (i, 0)
    fixed = lambda i: (0, 0)
    out_shape = [jax.ShapeDtypeStruct((n, D_ATTN), BF16)] * 3 + [jax.ShapeDtypeStruct((n, 3 * D_HYENA), F32)]
    out_specs = [pl.BlockSpec((tm, D_ATTN), row)] * 3 + [pl.BlockSpec((tm, 3 * D_HYENA), row)]
    if apply_ln:
        out_shape = [jax.ShapeDtypeStruct((n, D_MODEL), F32)] + out_shape
        out_specs = [pl.BlockSpec((tm, D_MODEL), row)] + out_specs
    return pl.pallas_call(
        functools.partial(_inproj_kernel, apply_ln),
        grid=(n // tm,),
        in_specs=[pl.BlockSpec((tm, D_MODEL), row), pl.BlockSpec((1, D_MODEL), fixed),
                  pl.BlockSpec((1, D_MODEL), fixed), pl.BlockSpec((D_MODEL, D_QKVH), fixed),
                  pl.BlockSpec((1, D_QKVH), fixed)],
        out_specs=out_specs,
        out_shape=out_shape,
        compiler_params=_params("parallel"),
        name="inproj",
    )(x, ln_g, ln_b, w, bias)


def _na_bias_table(rpb):
    t = np.arange(GRID_W)
    kc = np.arange(GRID_W)
    col_start = np.clip(t - NA_WIN_COLS // 2, 0, GRID_W - NA_WIN_COLS)
    valid = (kc[None, :] >= col_start[:, None]) & (kc[None, :] < col_start[:, None] + NA_WIN_COLS)
    col_off = np.clip(kc[None, :] - t[:, None], 1 - NA_WIN_COLS, NA_WIN_COLS - 1) + (NA_WIN_COLS - 1)
    n_off = 2 * NA_WIN_COLS - 1
    pick = (col_off[:, :, None] == np.arange(n_off)[None, None, :]).astype(np.float32)
    g = jnp.einsum('hjs,tks->htjk', rpb.astype(F32), pick, precision=lax.Precision.HIGHEST)
    g = jnp.where(jnp.asarray(valid)[None, :, None, :], g * LOG2E, NEG_BIG)
    top = NA_WIN_ROWS - 1
    g = jnp.stack([g[:, :, top - c:top - c + NA_WIN_ROWS, :] for c in range(NA_WIN_ROWS)], axis=0)
    return g.reshape(NA_WIN_ROWS, NA_HEADS * GRID_W, NA_WIN_ROWS * GRID_W)


def _head_mask():
    r = np.arange(NA_HEADS * GRID_W)[:, None] // GRID_W
    c = np.arange(D_ATTN)[None, :] // NA_HEAD_DIM
    return (r == c).astype(np.float32)


NA_ROWS_PER_STEP = 2


def _na_row(rows, r, q, k_ref, v_ref, bias, hm):
    first = jnp.clip(r - NA_WIN_ROWS // 2, 0, rows - NA_WIN_ROWS)
    start = pl.multiple_of(first * GRID_W, GRID_W)
    nkeys = NA_WIN_ROWS * GRID_W
    kw = k_ref[pl.ds(start, nkeys), :]
    vw = v_ref[pl.ds(start, nkeys), :]
    qbd = jnp.concatenate([q] * NA_HEADS, axis=0) * hm
    s = lax.dot_general(qbd, kw, (((1,), (1,)), ((), ())), preferred_element_type=F32) + bias
    e = jnp.exp2(s - jnp.max(s, axis=-1, keepdims=True))
    inv = 1.0 / jnp.sum(e, axis=-1, keepdims=True)
    p = e.astype(BF16)
    low_half = lax.broadcasted_iota(jnp.int32, (GRID_W, LANES), 1) < NA_HEAD_DIM
    tiles = []
    for j in range(D_ATTN // LANES):
        pair = slice(2 * j * GRID_W, (2 * j + 2) * GRID_W)
        o = jnp.dot(p[pair], vw[:, j * LANES:(j + 1) * LANES], preferred_element_type=F32) * inv[pair]
        tiles.append(jnp.where(low_half, o[0:GRID_W], o[GRID_W:]))
    return jnp.concatenate(tiles, axis=-1)


def _na_kernel(rows, q_ref, k_ref, v_ref, *rest):
    bias_refs, (hm_ref, o_ref) = rest[:NA_ROWS_PER_STEP], rest[NA_ROWS_PER_STEP:]
    hm = hm_ref[...]
    for j in range(NA_ROWS_PER_STEP):
        r = pl.program_id(1) * NA_ROWS_PER_STEP + j
        tok = slice(j * GRID_W, (j + 1) * GRID_W)
        o_ref[tok, :] = _na_row(rows, r, q_ref[tok, :], k_ref, v_ref, bias_refs[j][...], hm).astype(BF16)


def _neighbourhood_attention(q, k, v, bias_table, head_mask):
    b, l, _ = q.shape
    rows = l // GRID_W
    assert rows >= NA_WIN_ROWS and l % GRID_W == 0 and rows % NA_ROWS_PER_STEP == 0
    assert 2 * NA_HEAD_DIM == LANES
    nkeys = NA_WIN_ROWS * GRID_W
    nq = NA_HEADS * GRID_W
    tq = NA_ROWS_PER_STEP * GRID_W

    def cls(j):
        def index(bi, i):
            r = i * NA_ROWS_PER_STEP + j
            return (r - jnp.clip(r - NA_WIN_ROWS // 2, 0, rows - NA_WIN_ROWS), 0, 0)
        return index

    return pl.pallas_call(
        functools.partial(_na_kernel, rows),
        grid=(b, rows // NA_ROWS_PER_STEP),
        in_specs=[pl.BlockSpec((None, tq, D_ATTN), lambda bi, i: (bi, i, 0)),
                  pl.BlockSpec((None, l, D_ATTN), lambda bi, i: (bi, 0, 0)),
                  pl.BlockSpec((None, l, D_ATTN), lambda bi, i: (bi, 0, 0))]
                 + [pl.BlockSpec((None, nq, nkeys), cls(j)) for j in range(NA_ROWS_PER_STEP)]
                 + [pl.BlockSpec((nq, D_ATTN), lambda bi, i: (0, 0))],
        out_specs=pl.BlockSpec((None, tq, D_ATTN), lambda bi, i: (bi, i, 0)),
        out_shape=jax.ShapeDtypeStruct((b, l, D_ATTN), BF16),
        compiler_params=_params("parallel", "arbitrary"),
        name="nattn",
    )(q, k, v, *([bias_table] * NA_ROWS_PER_STEP), head_mask)


def _short_conv_kernel(nt, cur_ref, prev_ref, next_ref, sw_ref, sb_ref, u_ref, x0_ref):
    i = pl.program_id(1)
    cur = cur_ref[...]
    tl = cur.shape[0]
    row = lax.broadcasted_iota(jnp.int32, cur.shape, 0)
    prev_row = jnp.where(i > 0, prev_ref[7:8, :], 0.0)
    next_row = jnp.where(i < nt - 1, next_ref[0:1, :], 0.0)
    up = jnp.where(row == 0, prev_row, pltpu.roll(cur, 1, axis=0))
    dn = jnp.where(row == tl - 1, next_row, pltpu.roll(cur, tl - 1, axis=0))
    c = sw_ref[0:1, :] * up + sw_ref[1:2, :] * cur + sw_ref[2:3, :] * dn + sb_ref[...]
    x0 = c[:, 0:D_HYENA]
    x1 = c[:, D_HYENA:2 * D_HYENA]
    vv = c[:, 2 * D_HYENA:3 * D_HYENA]
    u_ref[...] = (vv * x1).astype(BF16)
    x0_ref[...] = x0.astype(BF16)


def _short_conv(hy, short_w, short_b):
    b, l, c3 = hy.shape
    tl = _row_tile(l, 512)
    nt = l // tl
    sub = tl // 8
    cur = lambda bi, i: (bi, i, 0)
    prev = lambda bi, i: (bi, jnp.maximum(i * sub - 1, 0), 0)
    nxt = lambda bi, i: (bi, jnp.minimum((i + 1) * sub, l // 8 - 1), 0)
    fixed = lambda bi, i: (0, 0)
    return pl.pallas_call(
        functools.partial(_short_conv_kernel, nt),
        grid=(b, nt),
        in_specs=[pl.BlockSpec((None, tl, c3), cur), pl.BlockSpec((None, 8, c3), prev),
                  pl.BlockSpec((None, 8, c3), nxt), pl.BlockSpec((3, c3), fixed),
                  pl.BlockSpec((1, c3), fixed)],
        out_specs=[pl.BlockSpec((None, tl, D_HYENA), cur)] * 2,
        out_shape=[jax.ShapeDtypeStruct((b, l, D_HYENA), BF16)] * 2,
        compiler_params=_params("parallel", "parallel"),
        name="short_conv",
    )(hy, hy, hy, short_w, short_b)


def _filter_features(l):
    t = jnp.linspace(0.0, 1.0, l, dtype=F32)[:, None]
    bands = (FILTER_EMB - 1) // 2
    f = jnp.linspace(1e-4, bands - 1, bands, dtype=F32)[None, :]
    ang = f * (2.0 * math.pi / l) * jnp.arange(l, dtype=F32)[:, None]
    z = jnp.concatenate([t, jnp.cos(ang), -jnp.sin(ang)], axis=-1)
    r = np.arange(2 * l)
    src = np.where(r < l, r, np.where(r == l, 0, 2 * l - r))
    z = z[src]
    return jnp.pad(z, ((0, 0), (0, LANES - FILTER_EMB)))


def _decay_rates():
    max_decay = math.log(DECAY_TARGET) / DECAY_FAST
    min_decay = math.log(DECAY_TARGET) / DECAY_SLOW
    return jnp.abs(jnp.linspace(min_decay, max_decay, D_HYENA, dtype=F32))[None, :]


def _filter_kernel(l, z_ref, w1_ref, b1_ref, w2_ref, b2_ref, w3_ref, b3_ref, w4_ref, freq_ref,
                   rate_ref, o_ref):
    hp = lax.Precision.HIGHEST
    z = z_ref[...]
    freq = freq_ref[...]
    h = jnp.sin(freq * (jnp.dot(z, w1_ref[...], precision=hp, preferred_element_type=F32) + b1_ref[...]))
    h = jnp.sin(freq * (jnp.dot(h, w2_ref[...], precision=hp, preferred_element_type=F32) + b2_ref[...]))
    h = jnp.sin(freq * (jnp.dot(h, w3_ref[...], precision=hp, preferred_element_type=F32) + b3_ref[...]))
    h = jnp.dot(h, w4_ref[...], precision=hp, preferred_element_type=F32)
    tr = z.shape[0]
    r = pl.program_id(0) * tr + lax.broadcasted_iota(jnp.int32, (tr, D_HYENA), 0)
    decay = jnp.exp(-z[:, 0:1] * rate_ref[...])
    tap = jnp.where(r < l, h[:, 0:D_HYENA], jnp.where(r > l, h[:, D_HYENA:], 0.0))
    o_ref[...] = (tap * decay * (0.5 / l)).astype(BF16)


def _filter_taps(l, z, w1, b1, w2, b2, w3, b3, w4, freq, rates):
    tr = _row_tile(2 * l, 512)
    fixed = lambda i: (0, 0)
    full = lambda a: pl.BlockSpec(a.shape, fixed)
    args = (w1, b1, w2, b2, w3, b3, w4, freq, rates)
    return pl.pallas_call(
        functools.partial(_filter_kernel, l),
        grid=(2 * l // tr,),
        in_specs=[pl.BlockSpec((tr, LANES), lambda i: (i, 0))] + [full(a) for a in args],
        out_specs=pl.BlockSpec((tr, D_HYENA), lambda i: (i, 0)),
        out_shape=jax.ShapeDtypeStruct((2 * l, D_HYENA), BF16),
        compiler_params=_params("parallel"),
        name="filter_taps",
    )(z, *args)


class _DftPlan:
    def __init__(self, l):
        n = 2 * l
        lg = n.bit_length() - 1
        assert 1 << lg == n
        self.n1 = n1 = 1 << (lg // 2)
        self.n2 = n2 = n // n1
        self.half = half = n1 // 2
        assert half % 16 == 0
        k1 = np.arange(n1)
        f1 = np.exp(-2j * np.pi * np.outer(k1, np.arange(n1)) / n1)
        f1h = f1[:, :half]
        self.a1_data = np.block([[f1h.real, -f1h.imag], [f1h.imag, f1h.real]]).astype(np.float32)
        self.a1_filt = np.concatenate([f1.real, f1.imag], axis=0).astype(np.float32)
        e = np.exp(2j * np.pi * np.outer(np.arange(half), k1) / n1)
        self.a3 = np.block([[e.real, -e.imag], [e.imag, e.real]]).astype(np.float32)
        tw = np.exp(-2j * np.pi * np.outer(k1, np.arange(n2)) / n)
        self.tw_r = tw.real.astype(np.float32)
        self.tw_i = tw.imag.astype(np.float32)
        f2 = np.exp(-2j * np.pi * np.outer(np.arange(n2), np.arange(n2)) / n2)
        self.f2_r = f2.real.astype(np.float32)
        self.f2_i = f2.imag.astype(np.float32)


def _colblock(cols):
    return min(cols, 8192)


def _dft_rows_kernel(a_ref, x_ref, o_ref):
    x = x_ref[...]
    x = x.reshape(x.shape[0] * x.shape[1], x.shape[2])
    o_ref[...] = jnp.dot(a_ref[...], x, preferred_element_type=F32).astype(BF16)


def _dft_rows(a, x):
    p, _, half, cols = x.shape
    m = a.shape[0]
    cb = _colblock(cols)
    return pl.pallas_call(
        _dft_rows_kernel,
        grid=(p, cols // cb),
        in_specs=[pl.BlockSpec(a.shape, lambda pi, j: (0, 0)),
                  pl.BlockSpec((None, 2, half, cb), lambda pi, j: (pi, 0, 0, j))],
        out_specs=pl.BlockSpec((None, m, cb), lambda pi, j: (pi, 0, j)),
        out_shape=jax.ShapeDtypeStruct((p, m, cols), BF16),
        compiler_params=_params("parallel", "parallel"),
        name="dft_rows",
    )(a, x)


def _twiddled_dft(f2r, f2i, twr, twi):
    gr = f2r * twr - f2i * twi
    gi = f2r * twi + f2i * twr
    return gr, gi


def _filter_spectrum_kernel(kb, z_ref, f2r_ref, f2i_ref, twr_ref, twi_ref, o_ref):
    f2r = f2r_ref[...]
    f2i = f2i_ref[...]

    def body(k, carry):
        gr, gi = _twiddled_dft(f2r, f2i, twr_ref[k], twi_ref[k])
        a_re = jnp.concatenate([gr, gi], axis=0).astype(BF16)
        a_im = jnp.concatenate([-gi, gr], axis=0).astype(BF16)
        y = (jnp.dot(a_re, z_ref[0, k], preferred_element_type=F32)
             + jnp.dot(a_im, z_ref[1, k], preferred_element_type=F32))
        n2 = f2r.shape[0]
        o_ref[0, k] = y[0:n2]
        o_ref[1, k] = y[n2:]
        return carry

    lax.fori_loop(0, kb, body, 0)


def _filter_spectrum(plan, z):
    _, n1, n2, c = z.shape
    kb = min(n1, 8)
    blk = lambda i: (0, i, 0, 0)
    fixed = lambda i: (0, 0)
    return pl.pallas_call(
        functools.partial(_filter_spectrum_kernel, kb),
        grid=(n1 // kb,),
        in_specs=[pl.BlockSpec((2, kb, n2, c), blk), pl.BlockSpec((n2, n2), fixed),
                  pl.BlockSpec((n2, n2), fixed), pl.BlockSpec((kb, 1, n2), lambda i: (i, 0, 0)),
                  pl.BlockSpec((kb, 1, n2), lambda i: (i, 0, 0))],
        out_specs=pl.BlockSpec((2, kb, n2, c), blk),
        out_shape=jax.ShapeDtypeStruct((2, n1, n2, c), F32),
        compiler_params=_params("parallel"),
        name="filter_spectrum",
    )(z, plan.f2_r, plan.f2_i, plan.tw_r[:, None, :], plan.tw_i[:, None, :])


def _spectral_kernel(kb, z_ref, kf_ref, f2r_ref, f2i_ref, twr_ref, twi_ref, twrc_ref, twic_ref, o_ref):
    f2r = f2r_ref[...]
    f2i = f2i_ref[...]
    n2 = f2r.shape[0]

    def body(k, carry):
        gr, gi = _twiddled_dft(f2r, f2i, twr_ref[k], twi_ref[k])
        a_re = jnp.concatenate([gr, gi], axis=0).astype(BF16)
        a_im = jnp.concatenate([-gi, gr], axis=0).astype(BF16)
        y = (jnp.dot(a_re, z_ref[0, k], preferred_element_type=F32)
             + jnp.dot(a_im, z_ref[1, k], preferred_element_type=F32))
        yr, yi = y[0:n2], y[n2:]
        kr, ki = kf_ref[0, k], kf_ref[1, k]
        wr = (yr * kr - yi * ki).astype(BF16)
        wi = (yr * ki + yi * kr).astype(BF16)
        cr, ci = _twiddled_dft(f2r, f2i, twrc_ref[k], twic_ref[k])
        b_re = jnp.concatenate([cr, -ci], axis=0).astype(BF16)
        b_im = jnp.concatenate([ci, cr], axis=0).astype(BF16)
        v = (jnp.dot(b_re, wr, preferred_element_type=F32)
             + jnp.dot(b_im, wi, preferred_element_type=F32))
        o_ref[0, k] = v[0:n2].astype(BF16)
        o_ref[1, k] = v[n2:].astype(BF16)
        return carry

    lax.fori_loop(0, kb, body, 0)


def _spectral_multiply(plan, z, kf):
    p, _, n1, n2, c = z.shape
    kb = min(n1, 8)
    zblk = lambda i, pi: (pi, 0, i, 0, 0)
    fixed = lambda i, pi: (0, 0)
    trow = pl.BlockSpec((kb, 1, n2), lambda i, pi: (i, 0, 0))
    tcol = pl.BlockSpec((kb, n2, 1), lambda i, pi: (i, 0, 0))
    return pl.pallas_call(
        functools.partial(_spectral_kernel, kb),
        grid=(n1 // kb, p),
        in_specs=[pl.BlockSpec((None, 2, kb, n2, c), zblk),
                  pl.BlockSpec((2, kb, n2, c), lambda i, pi: (0, i, 0, 0)),
                  pl.BlockSpec((n2, n2), fixed), pl.BlockSpec((n2, n2), fixed),
                  trow, trow, tcol, tcol],
        out_specs=pl.BlockSpec((None, 2, kb, n2, c), zblk),
        out_shape=jax.ShapeDtypeStruct(z.shape, BF16),
        compiler_params=_params("parallel", "arbitrary"),
        name="spectral_multiply",
    )(z, kf, plan.f2_r, plan.f2_i, plan.tw_r[:, None, :], plan.tw_i[:, None, :],
      plan.tw_r[:, :, None], plan.tw_i[:, :, None])


def _idft_rows_kernel(a_ref, v_ref, u_ref, bias_ref, o_ref):
    y = jnp.dot(a_ref[...], v_ref[...], preferred_element_type=F32)
    u = u_ref[...].astype(F32)
    o_ref[...] = (y.reshape(u.shape) + u * bias_ref[...]).astype(BF16)


def _idft_rows(a3, v, u, bias_cols):
    p, _, half, cols = u.shape
    cb = _colblock(cols)
    ublk = pl.BlockSpec((None, 2, half, cb), lambda pi, j: (pi, 0, 0, j))
    return pl.pallas_call(
        _idft_rows_kernel,
        grid=(p, cols // cb),
        in_specs=[pl.BlockSpec(a3.shape, lambda pi, j: (0, 0)),
                  pl.BlockSpec((None, v.shape[1], cb), lambda pi, j: (pi, 0, j)),
                  ublk, pl.BlockSpec((1, cb), lambda pi, j: (0, 0))],
        out_specs=ublk,
        out_shape=jax.ShapeDtypeStruct(u.shape, BF16),
        compiler_params=_params("parallel", "parallel"),
        name="idft_rows",
    )(a3, v, u, bias_cols)


def _hyena(hy, plan, kf, short_w, short_b, flt_bias):
    b, l, _ = hy.shape
    assert b % 2 == 0
    u, x0 = _short_conv(hy, short_w, short_b)
    cols = plan.n2 * D_HYENA
    pairs = lambda a: a.reshape(b // 2, 2, plan.half, cols)
    z = _dft_rows(jnp.asarray(plan.a1_data, BF16), pairs(u))
    z = z.reshape(b // 2, 2, plan.n1, plan.n2, D_HYENA)
    v = _spectral_multiply(plan, z, kf).reshape(b // 2, 2 * plan.n1, cols)
    bias_cols = jnp.tile(flt_bias.astype(F32)[None, :], (1, _colblock(cols) // D_HYENA))
    y = _idft_rows(jnp.asarray(plan.a3, BF16), v, pairs(u), bias_cols)
    return y.reshape(b * l, D_HYENA), x0.reshape(b * l, D_HYENA)


def _hyena_filter_spectrum(l, plan, z_feat, w1, b1, w2, b2, w3, b3, w4, freq):
    w1p = jnp.pad(w1, ((0, LANES - FILTER_EMB), (0, 0)))
    taps = _filter_taps(l, z_feat, w1p, b1[None, :], w2, b2[None, :], w3, b3[None, :], w4,
                        freq[None, :], _decay_rates())
    cols = plan.n2 * D_HYENA
    z = _dft_rows(jnp.asarray(plan.a1_filt, BF16), taps.reshape(1, 2, plan.half, cols))
    return _filter_spectrum(plan, z.reshape(2, plan.n1, plan.n2, D_HYENA))


def _sigmoid(x):
    return 1.0 / (1.0 + jnp.exp(-x))


def _merge_kernel(h_ref, oa_ref, yh_ref, x0_ref, wg_ref, bg_ref, wa_ref, wh_ref, wo_ref, g_ref, b_ref, o_ref):
    h = h_ref[...]
    hb = h.astype(BF16)
    ga = jnp.dot(hb, wg_ref[:, 0:D_MODEL], preferred_element_type=F32) + bg_ref[:, 0:D_MODEL]
    gh = jnp.dot(hb, wg_ref[:, D_MODEL:], preferred_element_type=F32) + bg_ref[:, D_MODEL:]
    o_hyena = yh_ref[...] * x0_ref[...]
    merged = (_sigmoid(ga) * jnp.dot(oa_ref[...], wa_ref[...], preferred_element_type=F32)
              + _sigmoid(gh) * jnp.dot(o_hyena, wh_ref[...], preferred_element_type=F32))
    mix = jnp.dot(merged.astype(BF16), wo_ref[...], preferred_element_type=F32)
    o_ref[...] = _layer_norm(ALPHA * h + mix, g_ref[...], b_ref[...])


def _merge(h, o_attn, y_hyena, x0, wg, bg, wa, wh, wo, ln_g, ln_b):
    n = h.shape[0]
    tm = _row_tile(n, 512)
    row = lambda i: (i, 0)
    fixed = lambda i: (0, 0)
    full = lambda a: pl.BlockSpec(a.shape, fixed)
    return pl.pallas_call(
        _merge_kernel,
        grid=(n // tm,),
        in_specs=[pl.BlockSpec((tm, D_MODEL), row), pl.BlockSpec((tm, D_ATTN), row),
                  pl.BlockSpec((tm, D_HYENA), row), pl.BlockSpec((tm, D_HYENA), row)]
                 + [full(a) for a in (wg, bg, wa, wh, wo, ln_g, ln_b)],
        out_specs=pl.BlockSpec((tm, D_MODEL), row),
        out_shape=jax.ShapeDtypeStruct((n, D_MODEL), F32),
        compiler_params=_params("parallel"),
        name="merge",
    )(h, o_attn, y_hyena, x0, wg, bg, wa, wh, wo, ln_g, ln_b)


ROUTE_TILE = 1024
ROW_TILE = 512
EXPERT_ROWS = 512
SLAB = 8
assert SLAB * LANES == D_MODEL
L_E1, L_E2, L_G1, L_G2, L_R1, L_R2 = 0, 1, 2, 3, 4, 5


def _router_kernel(x_ref, w_ref, b_ref, route_ref, count_ref):
    logits = jnp.dot(x_ref[...], w_ref[...], precision=lax.Precision.HIGHEST,
                     preferred_element_type=F32) + b_ref[...]
    tm = logits.shape[0]
    lane_i = lax.broadcasted_iota(jnp.int32, logits.shape, 1)
    lane = lane_i.astype(F32)
    group_of_lane = jnp.right_shift(lane_i, 2).astype(F32)
    neg = -jnp.inf

    def first_argmax(vals, vmax):
        return jnp.min(jnp.where(vals == vmax, lane, float(LANES)), axis=-1, keepdims=True)

    gl = jnp.where((lane_i >= N_EXPERTS) & (lane_i < N_EXPERTS + N_GROUPS), logits, neg)
    gmax = jnp.max(gl, axis=-1, keepdims=True)
    p_group = 1.0 / jnp.sum(jnp.exp(gl - gmax), axis=-1, keepdims=True)
    g_sel = first_argmax(gl, gmax) - float(N_EXPERTS)
    el = jnp.where((lane_i < N_EXPERTS) & (group_of_lane == g_sel), logits, neg)
    m1 = jnp.max(el, axis=-1, keepdims=True)
    i1 = first_argmax(el, m1)
    el2 = jnp.where(lane == i1, neg, el)
    m2 = jnp.max(el2, axis=-1, keepdims=True)
    i2 = first_argmax(el2, m2)
    e2 = jnp.exp(m2 - m1)
    denom = 1.0 / (1.0 + e2)
    gate1 = denom * p_group
    gate2 = e2 * denom * p_group
    hit1 = lane == i1
    hit2 = lane == i2
    onehot = jnp.where(hit1 | hit2, 1.0, 0.0)
    earlier = (lax.broadcasted_iota(jnp.int32, (tm, tm), 0) > lax.broadcasted_iota(jnp.int32, (tm, tm), 1))
    before = jnp.dot(jnp.where(earlier, 1.0, 0.0).astype(BF16), onehot.astype(BF16), preferred_element_type=F32)
    rank1 = jnp.sum(jnp.where(hit1, before, 0.0), axis=-1, keepdims=True)
    rank2 = jnp.sum(jnp.where(hit2, before, 0.0), axis=-1, keepdims=True)
    out = jnp.zeros_like(logits)
    for l, val in ((L_E1, i1), (L_E2, i2), (L_G1, gate1), (L_G2, gate2), (L_R1, rank1), (L_R2, rank2)):
        out = jnp.where(lane_i == l, val, out)
    route_ref[...] = out
    count_ref[...] = jnp.broadcast_to(jnp.sum(onehot, axis=0, keepdims=True), count_ref.shape)


def _router(x, w, b):
    n = x.shape[0]
    tm = _row_tile(n, ROUTE_TILE)
    nt = n // tm
    return pl.pallas_call(
        _router_kernel,
        grid=(nt,),
        in_specs=[pl.BlockSpec((tm, D_MODEL), lambda i: (i, 0)),
                  pl.BlockSpec((D_MODEL, LANES), lambda i: (0, 0)),
                  pl.BlockSpec((1, LANES), lambda i: (0, 0))],
        out_specs=[pl.BlockSpec((tm, LANES), lambda i: (i, 0)),
                   pl.BlockSpec((None, 8, LANES), lambda i: (i, 0, 0))],
        out_shape=[jax.ShapeDtypeStruct((n, LANES), F32), jax.ShapeDtypeStruct((nt, 8, LANES), F32)],
        compiler_params=_params("parallel"),
        name="router",
    )(x, w, b)


def _routing_tables(route, counts, n):
    tm = _row_tile(n, ROUTE_TILE)
    nt = n // tm
    cnt = counts[:, 0, :N_EXPERTS].astype(jnp.int32)
    total = jnp.sum(cnt, axis=0)
    padded = (total + EXPERT_ROWS - 1) // EXPERT_ROWS * EXPERT_ROWS
    ends = jnp.cumsum(padded)
    tile_base = (ends - padded)[None, :] + jnp.cumsum(cnt, axis=0) - cnt
    base_tok = jnp.broadcast_to(tile_base[:, None, :], (nt, tm, N_EXPERTS)).reshape(n, N_EXPERTS)
    experts = jnp.arange(N_EXPERTS, dtype=jnp.int32)[None, :]

    def position(l_e, l_r):
        e = route[:, l_e].astype(jnp.int32)
        return jnp.sum(jnp.where(e[:, None] == experts, base_tok, 0), axis=1) + route[:, l_r].astype(jnp.int32)

    pos = jnp.stack([position(L_E1, L_R1), position(L_E2, L_R2)], axis=1).reshape(2 * n)
    n_blocks = 2 * n // EXPERT_ROWS + N_EXPERTS
    block_start = jnp.arange(n_blocks, dtype=jnp.int32) * EXPERT_ROWS
    block_expert = jnp.sum((ends[None, :] <= block_start[:, None]).astype(jnp.int32), axis=1)
    block_valid = (block_expert < N_EXPERTS).astype(jnp.int32)
    last_block = jnp.where(padded > total, ends // EXPERT_ROWS - 1, -1)
    tail = ends[-1] // EXPERT_ROWS + jnp.arange(N_EXPERTS, dtype=jnp.int32)
    zero_blocks = jnp.concatenate([last_block, jnp.where(tail < n_blocks, tail, -1)]).astype(jnp.int32)
    return pos, jnp.minimum(block_expert, N_EXPERTS - 1), block_valid, zero_blocks, n_blocks


def _to_slabs(slab_ref, base, x):
    rows = x.shape[0]
    for s in range(SLAB):
        slab_ref[pl.ds(base + s, rows, stride=SLAB), :] = x[:, s * LANES:(s + 1) * LANES]


def _from_slabs(slab_ref, base, rows):
    return jnp.concatenate([slab_ref[pl.ds(base + s, rows, stride=SLAB), :] for s in range(SLAB)], axis=-1)


def _dispatch_kernel(pos_ref, zb_ref, x_ref, xs_ref, slab_ref, sem):
    tm = x_ref.shape[0]

    @pl.when(pl.program_id(0) == 0)
    def _():
        slab_ref[...] = jnp.zeros_like(slab_ref)
        block = slab_ref.shape[0]
        for j in range(zb_ref.shape[0]):
            @pl.when(zb_ref[j] >= 0)
            def _():
                pltpu.make_async_copy(slab_ref, xs_ref.at[pl.ds(zb_ref[j] * block, block)], sem).start()
        for j in range(zb_ref.shape[0]):
            @pl.when(zb_ref[j] >= 0)
            def _():
                pltpu.make_async_copy(slab_ref, xs_ref.at[pl.ds(0, block)], sem).wait()

    _to_slabs(slab_ref, 0, x_ref[...])

    def row_copy(t, k):
        p = pos_ref[2 * t + k]
        return pltpu.make_async_copy(slab_ref.at[pl.ds(t * SLAB, SLAB)], xs_ref.at[pl.ds(p * SLAB, SLAB)], sem)

    def issue(t, carry):
        row_copy(t, 0).start()
        row_copy(t, 1).start()
        return carry

    lax.fori_loop(0, tm, issue, 0)
    for _ in range(2):
        pltpu.make_async_copy(slab_ref, xs_ref.at[pl.ds(0, tm * SLAB)], sem).wait()


def _dispatch(x, pos, zero_blocks, n_rows):
    n = x.shape[0]
    tm = _row_tile(n, ROW_TILE)
    assert tm == EXPERT_ROWS
    return pl.pallas_call(
        _dispatch_kernel,
        grid=(n // tm,),
        in_specs=[pl.BlockSpec((2 * tm,), lambda i: (i,), memory_space=pltpu.SMEM),
                  pl.BlockSpec(zero_blocks.shape, lambda i: (0,), memory_space=pltpu.SMEM),
                  pl.BlockSpec((tm, D_MODEL), lambda i: (i, 0))],
        out_specs=pl.BlockSpec(memory_space=pl.ANY),
        out_shape=jax.ShapeDtypeStruct((n_rows * SLAB, LANES), F32),
        scratch_shapes=[pltpu.VMEM((tm * SLAB, LANES), F32), pltpu.SemaphoreType.DMA],
        compiler_params=_params("arbitrary"),
        name="dispatch",
    )(pos, zero_blocks, x)


def _expert_ffn_kernel(be_ref, bv_ref, xs_ref, w1_ref, w3_ref, w2_ref, y_ref):
    rows = xs_ref.shape[0] // SLAB
    valid = bv_ref[pl.program_id(0)] > 0

    @pl.when(valid)
    def _():
        xb = _from_slabs(xs_ref, 0, rows).astype(BF16)
        a = jnp.dot(xb, w1_ref[...], preferred_element_type=F32)
        g = jnp.dot(xb, w3_ref[...], preferred_element_type=F32)
        hidden = (a * _sigmoid(a) * g).astype(BF16)
        _to_slabs(y_ref, 0, jnp.dot(hidden, w2_ref[...], preferred_element_type=F32))

    @pl.when(jnp.logical_not(valid))
    def _():
        y_ref[...] = jnp.zeros_like(y_ref)


def _expert_ffn(xs, block_expert, block_valid, n_blocks, w1, w3, w2):
    rows = EXPERT_ROWS * SLAB
    blk = lambda b, be, bv: (b, 0)
    wsel = lambda b, be, bv: (be[b], 0, 0)
    return pl.pallas_call(
        _expert_ffn_kernel,
        grid_spec=pltpu.PrefetchScalarGridSpec(
            num_scalar_prefetch=2,
            grid=(n_blocks,),
            in_specs=[pl.BlockSpec((rows, LANES), blk),
                      pl.BlockSpec((None, D_MODEL, D_EXPERT), wsel),
                      pl.BlockSpec((None, D_MODEL, D_EXPERT), wsel),
                      pl.BlockSpec((None, D_EXPERT, D_MODEL), wsel)],
            out_specs=pl.BlockSpec((rows, LANES), blk)),
        out_shape=jax.ShapeDtypeStruct(xs.shape, F32),
        compiler_params=_params("arbitrary"),
        name="expert_ffn",
    )(block_expert, block_valid, xs, w1, w3, w2)


def _combine_kernel(pos_ref, x_ref, route_ref, g_ref, b_ref, y_ref, o_ref, slab_ref, sem):
    tm = x_ref.shape[0]

    def row_copy(t, k):
        p = pos_ref[2 * t + k]
        return pltpu.make_async_copy(y_ref.at[pl.ds(p * SLAB, SLAB)],
                                     slab_ref.at[pl.ds((k * tm + t) * SLAB, SLAB)], sem)

    def issue(t, carry):
        row_copy(t, 0).start()
        row_copy(t, 1).start()
        return carry

    lax.fori_loop(0, tm, issue, 0)
    for k in range(2):
        pltpu.make_async_copy(y_ref.at[pl.ds(0, tm * SLAB)], slab_ref.at[pl.ds(k * tm * SLAB, tm * SLAB)], sem).wait()
    route = route_ref[...]
    ffn = (route[:, L_G1:L_G1 + 1] * _from_slabs(slab_ref, 0, tm)
           + route[:, L_G2:L_G2 + 1] * _from_slabs(slab_ref, tm * SLAB, tm))
    o_ref[...] = _layer_norm(ALPHA * x_ref[...] + ffn, g_ref[...], b_ref[...])


def _combine(x, route, pos, y, ln_g, ln_b):
    n = x.shape[0]
    tm = _row_tile(n, ROW_TILE)
    row = lambda i: (i, 0)
    fixed = lambda i: (0, 0)
    return pl.pallas_call(
        _combine_kernel,
        grid=(n // tm,),
        in_specs=[pl.BlockSpec((2 * tm,), lambda i: (i,), memory_space=pltpu.SMEM),
                  pl.BlockSpec((tm, D_MODEL), row), pl.BlockSpec((tm, LANES), row),
                  pl.BlockSpec((1, D_MODEL), fixed), pl.BlockSpec((1, D_MODEL), fixed),
                  pl.BlockSpec(memory_space=pl.ANY)],
        out_specs=pl.BlockSpec((tm, D_MODEL), row),
        out_shape=jax.ShapeDtypeStruct((n, D_MODEL), F32),
        scratch_shapes=[pltpu.VMEM((2 * tm * SLAB, LANES), F32), pltpu.SemaphoreType.DMA],
        compiler_params=_params("arbitrary"),
        name="combine",
    )(pos, x, route, ln_g, ln_b, y)


def _moe(x, lay):
    n = x.shape[0]
    route, counts = _router(x, lay["w_route"], lay["b_route"])
    pos, block_expert, block_valid, zero_blocks, n_blocks = _routing_tables(route, counts, n)
    xs = _dispatch(x, pos, zero_blocks, n_blocks * EXPERT_ROWS)
    y = _expert_ffn(xs, block_expert, block_valid, n_blocks, lay["w1"], lay["w3"], lay["w2"])
    return _combine(x, route, pos, y, lay["ln2_g"], lay["ln2_b"])


def _prepare_layer(i, w_in, b_in, short_w, short_b, na_rpb, w_branch_attn, w_branch_hyena, w_out,
                   ln1_g, ln1_b, w_group, b_group, w_router, b_router, w1, w3, w2, ln2_g, ln2_b, flt_bias):
    row = lambda a: a[None, :].astype(F32)
    w_route = jnp.zeros((D_MODEL, LANES), F32)
    w_route = w_route.at[:, 0:N_EXPERTS].set(w_router[i]).at[:, N_EXPERTS:N_EXPERTS + N_GROUPS].set(w_group[i])
    b_route = jnp.zeros((1, LANES), F32)
    b_route = b_route.at[0, 0:N_EXPERTS].set(b_router[i]).at[0, N_EXPERTS:N_EXPERTS + N_GROUPS].set(b_group[i])
    return dict(
        w_qkvh=w_in[i][:, :D_QKVH].astype(BF16), b_qkvh=row(b_in[i][:D_QKVH]),
        w_gates=w_in[i][:, D_QKVH:].astype(BF16), b_gates=row(b_in[i][D_QKVH:]),
        short_w=short_w[i], short_b=row(short_b[i]), flt_bias=flt_bias[i],
        na_bias=_na_bias_table(na_rpb[i]),
        wa=w_branch_attn[i].astype(BF16), wh=w_branch_hyena[i].astype(BF16), wo=w_out[i].astype(BF16),
        ln1_g=row(ln1_g[i]), ln1_b=row(ln1_b[i]), ln2_g=row(ln2_g[i]), ln2_b=row(ln2_b[i]),
        w_route=w_route, b_route=b_route,
        w1=w1[i].astype(BF16), w3=w3[i].astype(BF16), w2=w2[i].astype(BF16),
    )


def _trunk(x, ln_in_g, ln_in_b, layers, filt):
    b, l, d = x.shape
    n = b * l
    plan = _DftPlan(l)
    z_feat = _filter_features(l)
    head_mask = jnp.asarray(_head_mask(), BF16)
    h = x.reshape(n, d)
    for i, lay in enumerate(layers):
        outs = _inproj(h, ln_in_g[None, :], ln_in_b[None, :], lay["w_qkvh"], lay["b_qkvh"], apply_ln=(i == 0))
        if i == 0:
            h, q, k, v, hy = outs
        else:
            q, k, v, hy = outs
        seq = lambda a: a.reshape(b, l, a.shape[-1])
        o_attn = _neighbourhood_attention(seq(q), seq(k), seq(v), lay["na_bias"], head_mask).reshape(n, D_ATTN)
        kf = _hyena_filter_spectrum(l, plan, z_feat, *[f[i] for f in filt])
        y_hyena, x0 = _hyena(seq(hy), plan, kf, lay["short_w"], lay["short_b"], lay["flt_bias"])
        h = _merge(h, o_attn, y_hyena, x0, lay["w_gates"], lay["b_gates"], lay["wa"], lay["wh"], lay["wo"],
                   lay["ln1_g"], lay["ln1_b"])
        h = _moe(h, lay)
    return h.reshape(b, l, d)


def kernel(x_prompt, x_sample, ln_in_g, ln_in_b, w_in, b_in, short_w, short_b, na_rpb, flt_w1, flt_b1, flt_w2, flt_b2, flt_w3, flt_b3, flt_w4, flt_freq, flt_bias, w_branch_attn, w_branch_hyena, w_out, ln1_g, ln1_b, w_group, b_group, w_router, b_router, w1, w3, w2, ln2_g, ln2_b):
    layers = [_prepare_layer(i, w_in, b_in, short_w, short_b, na_rpb, w_branch_attn, w_branch_hyena, w_out,
                             ln1_g, ln1_b, w_group, b_group, w_router, b_router, w1, w3, w2, ln2_g, ln2_b,
                             flt_bias) for i in range(DEPTH)]
    filt = (flt_w1, flt_b1, flt_w2, flt_b2, flt_w3, flt_b3, flt_w4, flt_freq)
    y_prompt = _trunk(x_prompt, ln_in_g, ln_in_b, layers, filt)
    y_sample = _trunk(x_sample, ln_in_g, ln_in_b, layers, filt)
    return (y_prompt, y_sample)
```

```python
import functools
import math

import numpy as np
import jax
import jax.numpy as jnp
from jax import lax
from jax.experimental import pallas as pl
from jax.experimental.pallas import tpu as pltpu

F32 = jnp.float32
BF16 = jnp.bfloat16

D_MODEL = 1024
DEPTH = 2
GRID_W = 64
NA_HEADS = 8
NA_HEAD_DIM = 64
D_ATTN = NA_HEADS * NA_HEAD_DIM
NA_WIN_ROWS = 8
NA_WIN_COLS = 16
D_HYENA = D_MODEL // 2
FILTER_EMB = 33
FILTER_HIDDEN = 64
DECAY_FAST = 0.3
DECAY_SLOW = 1.5
DECAY_TARGET = 1e-2
N_GROUPS = 4
EXPERTS_PER_GROUP = 4
N_EXPERTS = N_GROUPS * EXPERTS_PER_GROUP
D_EXPERT = 512
LN_EPS = 1e-5
ALPHA = (2.0 * DEPTH) ** 0.25
D_QKVH = 3 * D_ATTN + 3 * D_HYENA
D_GATES = 2 * D_MODEL

LANES = 128
VMEM_LIMIT = 56 * 1024 * 1024
NEG_BIG = -1e30
LOG2E = math.log2(math.e)


def _params(*sem):
    return pltpu.CompilerParams(dimension_semantics=sem, vmem_limit_bytes=VMEM_LIMIT)


def _layer_norm(x, g, b):
    mu = jnp.mean(x, axis=-1, keepdims=True)
    xc = x - mu
    var = jnp.mean(xc * xc, axis=-1, keepdims=True)
    return xc * lax.rsqrt(var + LN_EPS) * g + b


def _row_tile(n, want):
    t = min(n, want)
    assert n % t == 0, (n, t)
    return t


HALO = 8


def _inproj_kernel(apply_ln, seq_tiles, x_ref, prev_ref, next_ref, g_ref, b_ref, w_ref, bias_ref,
                   sw_ref, sb_ref, *out_refs):
    x = x_ref[...]
    halo = jnp.concatenate([prev_ref[...], next_ref[...]], axis=0)
    if apply_ln:
        h_ref, q_ref, k_ref, v_ref, u_ref, x0_ref = out_refs
        x = _layer_norm(x, g_ref[...], b_ref[...])
        halo = _layer_norm(halo, g_ref[...], b_ref[...])
        h_ref[...] = x
    else:
        q_ref, k_ref, v_ref, u_ref, x0_ref = out_refs
    xb = x.astype(BF16)
    tm = xb.shape[0]

    def proj(rows, lo, hi):
        return jnp.dot(rows, w_ref[:, lo:hi], preferred_element_type=F32) + bias_ref[:, lo:hi]

    q_ref[...] = (proj(xb, 0, D_ATTN) * (NA_HEAD_DIM ** -0.5 * LOG2E)).astype(BF16)
    k_ref[...] = proj(xb, D_ATTN, 2 * D_ATTN).astype(BF16)
    v_ref[...] = proj(xb, 2 * D_ATTN, 3 * D_ATTN).astype(BF16)

    hy = proj(jnp.concatenate([xb, halo.astype(BF16)], axis=0), 3 * D_ATTN, D_QKVH)
    cur = hy[0:tm]
    i = pl.program_id(0) % seq_tiles
    prev_row = jnp.where(i > 0, hy[tm + HALO - 1:tm + HALO], 0.0)
    next_row = jnp.where(i < seq_tiles - 1, hy[tm + HALO:tm + HALO + 1], 0.0)
    row = lax.broadcasted_iota(jnp.int32, cur.shape, 0)
    up = jnp.where(row == 0, prev_row, pltpu.roll(cur, 1, axis=0))
    dn = jnp.where(row == tm - 1, next_row, pltpu.roll(cur, tm - 1, axis=0))
    c = sw_ref[0:1, :] * up + sw_ref[1:2, :] * cur + sw_ref[2:3, :] * dn + sb_ref[...]
    u_ref[...] = (c[:, 2 * D_HYENA:] * c[:, D_HYENA:2 * D_HYENA]).astype(BF16)
    x0_ref[...] = c[:, 0:D_HYENA].astype(BF16)


def _inproj(x, seq_len, ln_g, ln_b, w, bias, short_w, short_b, apply_ln):
    n = x.shape[0]
    tm = _row_tile(seq_len, 512)
    sub = tm // HALO
    row = lambda i: (i, 0)
    fixed = lambda i: (0, 0)
    prev = lambda i: (jnp.maximum(i * sub - 1, 0), 0)
    nxt = lambda i: (jnp.minimum((i + 1) * sub, n // HALO - 1), 0)
    out_shape = [jax.ShapeDtypeStruct((n, D_ATTN), BF16)] * 3 + [jax.ShapeDtypeStruct((n, D_HYENA), BF16)] * 2
    out_specs = [pl.BlockSpec((tm, D_ATTN), row)] * 3 + [pl.BlockSpec((tm, D_HYENA), row)] * 2
    if apply_ln:
        out_shape = [jax.ShapeDtypeStruct((n, D_MODEL), F32)] + out_shape
        out_specs = [pl.BlockSpec((tm, D_MODEL), row)] + out_specs
    return pl.pallas_call(
        functools.partial(_inproj_kernel, apply_ln, seq_len // tm),
        grid=(n // tm,),
        in_specs=[pl.BlockSpec((tm, D_MODEL), row), pl.BlockSpec((HALO, D_MODEL), prev),
                  pl.BlockSpec((HALO, D_MODEL), nxt), pl.BlockSpec((1, D_MODEL), fixed),
                  pl.BlockSpec((1, D_MODEL), fixed), pl.BlockSpec((D_MODEL, D_QKVH), fixed),
                  pl.BlockSpec((1, D_QKVH), fixed), pl.BlockSpec((3, 3 * D_HYENA), fixed),
                  pl.BlockSpec((1, 3 * D_HYENA), fixed)],
        out_specs=out_specs,
        out_shape=out_shape,
        compiler_params=_params("parallel"),
        name="inproj",
    )(x, x, x, ln_g, ln_b, w, bias, short_w, short_b)


def _na_bias_table(rpb):
    t = np.arange(GRID_W)
    kc = np.arange(GRID_W)
    col_start = np.clip(t - NA_WIN_COLS // 2, 0, GRID_W - NA_WIN_COLS)
    valid = (kc[None, :] >= col_start[:, None]) & (kc[None, :] < col_start[:, None] + NA_WIN_COLS)
    col_off = np.clip(kc[None, :] - t[:, None], 1 - NA_WIN_COLS, NA_WIN_COLS - 1) + (NA_WIN_COLS - 1)
    n_off = 2 * NA_WIN_COLS - 1
    pick = (col_off[:, :, None] == np.arange(n_off)[None, None, :]).astype(np.float32)
    g = jnp.einsum('hjs,tks->htjk', rpb.astype(F32), pick, precision=lax.Precision.HIGHEST)
    g = jnp.where(jnp.asarray(valid)[None, :, None, :], g * LOG2E, NEG_BIG)
    top = NA_WIN_ROWS - 1
    g = jnp.stack([g[:, :, top - c:top - c + NA_WIN_ROWS, :] for c in range(NA_WIN_ROWS)], axis=0)
    return g.reshape(NA_WIN_ROWS, NA_HEADS * GRID_W, NA_WIN_ROWS * GRID_W)


def _head_mask():
    r = np.arange(NA_HEADS * GRID_W)[:, None] // GRID_W
    c = np.arange(D_ATTN)[None, :] // NA_HEAD_DIM
    return (r == c).astype(np.float32)


NA_ROWS_PER_STEP = 2


def _na_row(rows, r, q, k_ref, v_ref, bias, hm):
    first = jnp.clip(r - NA_WIN_ROWS // 2, 0, rows - NA_WIN_ROWS)
    start = pl.multiple_of(first * GRID_W, GRID_W)
    nkeys = NA_WIN_ROWS * GRID_W
    kw = k_ref[pl.ds(start, nkeys), :]
    vw = v_ref[pl.ds(start, nkeys), :]
    qbd = jnp.concatenate([q] * NA_HEADS, axis=0) * hm
    s = lax.dot_general(qbd, kw, (((1,), (1,)), ((), ())), preferred_element_type=F32) + bias
    e = jnp.exp2(s - jnp.max(s, axis=-1, keepdims=True))
    inv = 1.0 / jnp.sum(e, axis=-1, keepdims=True)
    p = e.astype(BF16)
    low_half = lax.broadcasted_iota(jnp.int32, (GRID_W, LANES), 1) < NA_HEAD_DIM
    tiles = []
    for j in range(D_ATTN // LANES):
        pair = slice(2 * j * GRID_W, (2 * j + 2) * GRID_W)
        o = jnp.dot(p[pair], vw[:, j * LANES:(j + 1) * LANES], preferred_element_type=F32) * inv[pair]
        tiles.append(jnp.where(low_half, o[0:GRID_W], o[GRID_W:]))
    return jnp.concatenate(tiles, axis=-1)


def _na_kernel(rows, q_ref, k_ref, v_ref, *rest):
    bias_refs, (hm_ref, o_ref) = rest[:NA_ROWS_PER_STEP], rest[NA_ROWS_PER_STEP:]
    hm = hm_ref[...]
    for j in range(NA_ROWS_PER_STEP):
        r = pl.program_id(1) * NA_ROWS_PER_STEP + j
        tok = slice(j * GRID_W, (j + 1) * GRID_W)
        o_ref[tok, :] = _na_row(rows, r, q_ref[tok, :], k_ref, v_ref, bias_refs[j][...], hm).astype(BF16)


def _neighbourhood_attention(q, k, v, bias_table, head_mask):
    b, l, _ = q.shape
    rows = l // GRID_W
    assert rows >= NA_WIN_ROWS and l % GRID_W == 0 and rows % NA_ROWS_PER_STEP == 0
    assert 2 * NA_HEAD_DIM == LANES
    nkeys = NA_WIN_ROWS * GRID_W
    nq = NA_HEADS * GRID_W
    tq = NA_ROWS_PER_STEP * GRID_W

    def cls(j):
        def index(bi, i):
            r = i * NA_ROWS_PER_STEP + j
            return (r - jnp.clip(r - NA_WIN_ROWS // 2, 0, rows - NA_WIN_ROWS), 0, 0)
        return index

    return pl.pallas_call(
        functools.partial(_na_kernel, rows),
        grid=(b, rows // NA_ROWS_PER_STEP),
        in_specs=[pl.BlockSpec((None, tq, D_ATTN), lambda bi, i: (bi, i, 0)),
                  pl.BlockSpec((None, l, D_ATTN), lambda bi, i: (bi, 0, 0)),
                  pl.BlockSpec((None, l, D_ATTN), lambda bi, i: (bi, 0, 0))]
                 + [pl.BlockSpec((None, nq, nkeys), cls(j)) for j in range(NA_ROWS_PER_STEP)]
                 + [pl.BlockSpec((nq, D_ATTN), lambda bi, i: (0, 0))],
        out_specs=pl.BlockSpec((None, tq, D_ATTN), lambda bi, i: (bi, i, 0)),
        out_shape=jax.ShapeDtypeStruct((b, l, D_ATTN), BF16),
        compiler_params=_params("parallel", "arbitrary"),
        name="nattn",
    )(q, k, v, *([bias_table] * NA_ROWS_PER_STEP), head_mask)


def _filter_features(l):
    t = jnp.linspace(0.0, 1.0, l, dtype=F32)[:, None]
    bands = (FILTER_EMB - 1) // 2
    f = jnp.linspace(1e-4, bands - 1, bands, dtype=F32)[None, :]
    ang = f * (2.0 * math.pi / l) * jnp.arange(l, dtype=F32)[:, None]
    z = jnp.concatenate([t, jnp.cos(ang), -jnp.sin(ang)], axis=-1)
    r = np.arange(2 * l)
    src = np.where(r < l, r, np.where(r == l, 0, 2 * l - r))
    z = z[src]
    return jnp.pad(z, ((0, 0), (0, LANES - FILTER_EMB)))


def _decay_rates():
    max_decay = math.log(DECAY_TARGET) / DECAY_FAST
    min_decay = math.log(DECAY_TARGET) / DECAY_SLOW
    return jnp.abs(jnp.linspace(min_decay, max_decay, D_HYENA, dtype=F32))[None, :]


def _filter_kernel(l, z_ref, w1_ref, b1_ref, w2_ref, b2_ref, w3_ref, b3_ref, w4_ref, freq_ref,
                   rate_ref, o_ref):
    hp = lax.Precision.HIGHEST
    z = z_ref[...]
    freq = freq_ref[...]
    h = jnp.sin(freq * (jnp.dot(z, w1_ref[...], precision=hp, preferred_element_type=F32) + b1_ref[...]))
    h = jnp.sin(freq * (jnp.dot(h, w2_ref[...], precision=hp, preferred_element_type=F32) + b2_ref[...]))
    h = jnp.sin(freq * (jnp.dot(h, w3_ref[...], precision=hp, preferred_element_type=F32) + b3_ref[...]))
    w4 = w4_ref[...]
    h_hi, w_hi = h.astype(BF16), w4.astype(BF16)
    h_lo, w_lo = (h - h_hi.astype(F32)).astype(BF16), (w4 - w_hi.astype(F32)).astype(BF16)
    h = (jnp.dot(h_hi, w_hi, preferred_element_type=F32) + jnp.dot(h_lo, w_hi, preferred_element_type=F32)
         + jnp.dot(h_hi, w_lo, preferred_element_type=F32))
    tr = z.shape[0]
    r = pl.program_id(0) * tr + lax.broadcasted_iota(jnp.int32, (tr, D_HYENA), 0)
    decay = jnp.exp(-z[:, 0:1] * rate_ref[...])
    tap = jnp.where(r < l, h[:, 0:D_HYENA], jnp.where(r > l, h[:, D_HYENA:], 0.0))
    o_ref[...] = (tap * decay * (0.5 / l)).astype(BF16)


def _filter_taps(l, z, w1, b1, w2, b2, w3, b3, w4, freq, rates):
    tr = _row_tile(2 * l, 512)
    fixed = lambda i: (0, 0)
    full = lambda a: pl.BlockSpec(a.shape, fixed)
    args = (w1, b1, w2, b2, w3, b3, w4, freq, rates)
    return pl.pallas_call(
        functools.partial(_filter_kernel, l),
        grid=(2 * l // tr,),
        in_specs=[pl.BlockSpec((tr, LANES), lambda i: (i, 0))] + [full(a) for a in args],
        out_specs=pl.BlockSpec((tr, D_HYENA), lambda i: (i, 0)),
        out_shape=jax.ShapeDtypeStruct((2 * l, D_HYENA), BF16),
        compiler_params=_params("parallel"),
        name="filter_taps",
    )(z, *args)


class _DftPlan:
    def __init__(self, l):
        n = 2 * l
        lg = n.bit_length() - 1
        assert 1 << lg == n
        self.n1 = n1 = 1 << (lg // 2)
        self.n2 = n2 = n // n1
        self.half = half = n1 // 2
        assert half % 16 == 0
        k1 = np.arange(n1)
        f1 = np.exp(-2j * np.pi * np.outer(k1, np.arange(n1)) / n1)
        f1h = f1[:, :half]
        self.a1_data = np.block([[f1h.real, -f1h.imag], [f1h.imag, f1h.real]]).astype(np.float32)
        self.a1_filt = np.concatenate([f1.real, f1.imag], axis=0).astype(np.float32)
        e = np.exp(2j * np.pi * np.outer(np.arange(half), k1) / n1)
        self.a3 = np.block([[e.real, -e.imag], [e.imag, e.real]]).astype(np.float32)
        tw = np.exp(-2j * np.pi * np.outer(k1, np.arange(n2)) / n)
        self.tw_r = tw.real.astype(np.float32)
        self.tw_i = tw.imag.astype(np.float32)
        f2 = np.exp(-2j * np.pi * np.outer(np.arange(n2), np.arange(n2)) / n2)
        self.f2_r = f2.real.astype(np.float32)
        self.f2_i = f2.imag.astype(np.float32)


def _colblock(cols):
    return min(cols, 8192)


def _dft_rows_kernel(a_ref, x_ref, o_ref):
    x = x_ref[...]
    x = x.reshape(x.shape[0] * x.shape[1], x.shape[2])
    o_ref[...] = jnp.dot(a_ref[...], x, preferred_element_type=F32).astype(BF16)


def _dft_rows(a, x):
    p, _, half, cols = x.shape
    m = a.shape[0]
    cb = _colblock(cols)
    return pl.pallas_call(
        _dft_rows_kernel,
        grid=(p, cols // cb),
        in_specs=[pl.BlockSpec(a.shape, lambda pi, j: (0, 0)),
                  pl.BlockSpec((None, 2, half, cb), lambda pi, j: (pi, 0, 0, j))],
        out_specs=pl.BlockSpec((None, m, cb), lambda pi, j: (pi, 0, j)),
        out_shape=jax.ShapeDtypeStruct((p, m, cols), BF16),
        compiler_params=_params("parallel", "parallel"),
        name="dft_rows",
    )(a, x)


def _twiddled_dft(f2r, f2i, twr, twi):
    gr = f2r * twr - f2i * twi
    gi = f2r * twi + f2i * twr
    return gr, gi


def _filter_spectrum_kernel(kb, z_ref, f2r_ref, f2i_ref, twr_ref, twi_ref, o_ref):
    f2r = f2r_ref[...]
    f2i = f2i_ref[...]

    def body(k, carry):
        gr, gi = _twiddled_dft(f2r, f2i, twr_ref[k], twi_ref[k])
        a_re = jnp.concatenate([gr, gi], axis=0).astype(BF16)
        a_im = jnp.concatenate([-gi, gr], axis=0).astype(BF16)
        y = (jnp.dot(a_re, z_ref[0, k], preferred_element_type=F32)
             + jnp.dot(a_im, z_ref[1, k], preferred_element_type=F32))
        n2 = f2r.shape[0]
        o_ref[0, k] = y[0:n2]
        o_ref[1, k] = y[n2:]
        return carry

    lax.fori_loop(0, kb, body, 0, unroll=True)


def _filter_spectrum(plan, z):
    _, n1, n2, c = z.shape
    kb = min(n1, 8)
    blk = lambda i: (0, i, 0, 0)
    fixed = lambda i: (0, 0)
    return pl.pallas_call(
        functools.partial(_filter_spectrum_kernel, kb),
        grid=(n1 // kb,),
        in_specs=[pl.BlockSpec((2, kb, n2, c), blk), pl.BlockSpec((n2, n2), fixed),
                  pl.BlockSpec((n2, n2), fixed), pl.BlockSpec((kb, 1, n2), lambda i: (i, 0, 0)),
                  pl.BlockSpec((kb, 1, n2), lambda i: (i, 0, 0))],
        out_specs=pl.BlockSpec((2, kb, n2, c), blk),
        out_shape=jax.ShapeDtypeStruct((2, n1, n2, c), F32),
        compiler_params=_params("parallel"),
        name="filter_spectrum",
    )(z, plan.f2_r, plan.f2_i, plan.tw_r[:, None, :], plan.tw_i[:, None, :])


def _spectral_kernel(kb, z_ref, kf_ref, f2r_ref, f2i_ref, twr_ref, twi_ref, twrc_ref, twic_ref, o_ref):
    f2r = f2r_ref[...]
    f2i = f2i_ref[...]
    n2 = f2r.shape[0]

    def body(k, carry):
        gr, gi = _twiddled_dft(f2r, f2i, twr_ref[k], twi_ref[k])
        a_re = jnp.concatenate([gr, gi], axis=0).astype(BF16)
        a_im = jnp.concatenate([-gi, gr], axis=0).astype(BF16)
        y = (jnp.dot(a_re, z_ref[0, k], preferred_element_type=F32)
             + jnp.dot(a_im, z_ref[1, k], preferred_element_type=F32))
        yr, yi = y[0:n2], y[n2:]
        kr, ki = kf_ref[0, k], kf_ref[1, k]
        wr = (yr * kr - yi * ki).astype(BF16)
        wi = (yr * ki + yi * kr).astype(BF16)
        cr, ci = _twiddled_dft(f2r, f2i, twrc_ref[k], twic_ref[k])
        b_re = jnp.concatenate([cr, -ci], axis=0).astype(BF16)
        b_im = jnp.concatenate([ci, cr], axis=0).astype(BF16)
        v = (jnp.dot(b_re, wr, preferred_element_type=F32)
             + jnp.dot(b_im, wi, preferred_element_type=F32))
        o_ref[0, k] = v[0:n2].astype(BF16)
        o_ref[1, k] = v[n2:].astype(BF16)
        return carry

    lax.fori_loop(0, kb, body, 0, unroll=True)


def _spectral_multiply(plan, z, kf):
    p, _, n1, n2, c = z.shape
    kb = min(n1, 8)
    zblk = lambda i, pi: (pi, 0, i, 0, 0)
    fixed = lambda i, pi: (0, 0)
    trow = pl.BlockSpec((kb, 1, n2), lambda i, pi: (i, 0, 0))
    tcol = pl.BlockSpec((kb, n2, 1), lambda i, pi: (i, 0, 0))
    return pl.pallas_call(
        functools.partial(_spectral_kernel, kb),
        grid=(n1 // kb, p),
        in_specs=[pl.BlockSpec((None, 2, kb, n2, c), zblk),
                  pl.BlockSpec((2, kb, n2, c), lambda i, pi: (0, i, 0, 0)),
                  pl.BlockSpec((n2, n2), fixed), pl.BlockSpec((n2, n2), fixed),
                  trow, trow, tcol, tcol],
        out_specs=pl.BlockSpec((None, 2, kb, n2, c), zblk),
        out_shape=jax.ShapeDtypeStruct(z.shape, BF16),
        compiler_params=_params("parallel", "arbitrary"),
        name="spectral_multiply",
    )(z, kf, plan.f2_r, plan.f2_i, plan.tw_r[:, None, :], plan.tw_i[:, None, :],
      plan.tw_r[:, :, None], plan.tw_i[:, :, None])


def _idft_rows_kernel(a_ref, v_ref, u_ref, bias_ref, o_ref):
    y = jnp.dot(a_ref[...], v_ref[...], preferred_element_type=F32)
    u = u_ref[...].astype(F32)
    o_ref[...] = (y.reshape(u.shape) + u * bias_ref[...]).astype(BF16)


def _idft_rows(a3, v, u, bias_cols):
    p, _, half, cols = u.shape
    cb = _colblock(cols)
    ublk = pl.BlockSpec((None, 2, half, cb), lambda pi, j: (pi, 0, 0, j))
    return pl.pallas_call(
        _idft_rows_kernel,
        grid=(p, cols // cb),
        in_specs=[pl.BlockSpec(a3.shape, lambda pi, j: (0, 0)),
                  pl.BlockSpec((None, v.shape[1], cb), lambda pi, j: (pi, 0, j)),
                  ublk, pl.BlockSpec((1, cb), lambda pi, j: (0, 0))],
        out_specs=ublk,
        out_shape=jax.ShapeDtypeStruct(u.shape, BF16),
        compiler_params=_params("parallel", "parallel"),
        name="idft_rows",
    )(a3, v, u, bias_cols)


def _hyena_long_conv(u, plan, kf, flt_bias):
    b, l, _ = u.shape
    assert b % 2 == 0
    cols = plan.n2 * D_HYENA
    pairs = lambda a: a.reshape(b // 2, 2, plan.half, cols)
    z = _dft_rows(jnp.asarray(plan.a1_data, BF16), pairs(u))
    z = z.reshape(b // 2, 2, plan.n1, plan.n2, D_HYENA)
    v = _spectral_multiply(plan, z, kf).reshape(b // 2, 2 * plan.n1, cols)
    bias_cols = jnp.tile(flt_bias.astype(F32)[None, :], (1, _colblock(cols) // D_HYENA))
    y = _idft_rows(jnp.asarray(plan.a3, BF16), v, pairs(u), bias_cols)
    return y.reshape(b * l, D_HYENA)


def _hyena_filter_spectrum(l, plan, z_feat, w1, b1, w2, b2, w3, b3, w4, freq):
    w1p = jnp.pad(w1, ((0, LANES - FILTER_EMB), (0, 0)))
    taps = _filter_taps(l, z_feat, w1p, b1[None, :], w2, b2[None, :], w3, b3[None, :], w4,
                        freq[None, :], _decay_rates())
    cols = plan.n2 * D_HYENA
    z = _dft_rows(jnp.asarray(plan.a1_filt, BF16), taps.reshape(1, 2, plan.half, cols))
    return _filter_spectrum(plan, z.reshape(2, plan.n1, plan.n2, D_HYENA))


def _sigmoid(x):
    return 1.0 / (1.0 + jnp.exp(-x))


def _merge_kernel(h_ref, oa_ref, yh_ref, x0_ref, wg_ref, bg_ref, wa_ref, wh_ref, wo_ref, g_ref, b_ref, o_ref):
    h = h_ref[...]
    hb = h.astype(BF16)
    ga = jnp.dot(hb, wg_ref[:, 0:D_MODEL], preferred_element_type=F32) + bg_ref[:, 0:D_MODEL]
    gh = jnp.dot(hb, wg_ref[:, D_MODEL:], preferred_element_type=F32) + bg_ref[:, D_MODEL:]
    o_hyena = yh_ref[...] * x0_ref[...]
    merged = (_sigmoid(ga) * jnp.dot(oa_ref[...], wa_ref[...], preferred_element_type=F32)
              + _sigmoid(gh) * jnp.dot(o_hyena, wh_ref[...], preferred_element_type=F32))
    mix = jnp.dot(merged.astype(BF16), wo_ref[...], preferred_element_type=F32)
    o_ref[...] = _layer_norm(ALPHA * h + mix, g_ref[...], b_ref[...])


def _merge(h, o_attn, y_hyena, x0, wg, bg, wa, wh, wo, ln_g, ln_b):
    n = h.shape[0]
    tm = _row_tile(n, 512)
    row = lambda i: (i, 0)
    fixed = lambda i: (0, 0)
    full = lambda a: pl.BlockSpec(a.shape, fixed)
    return pl.pallas_call(
        _merge_kernel,
        grid=(n // tm,),
        in_specs=[pl.BlockSpec((tm, D_MODEL), row), pl.BlockSpec((tm, D_ATTN), row),
                  pl.BlockSpec((tm, D_HYENA), row), pl.BlockSpec((tm, D_HYENA), row)]
                 + [full(a) for a in (wg, bg, wa, wh, wo, ln_g, ln_b)],
        out_specs=pl.BlockSpec((tm, D_MODEL), row),
        out_shape=jax.ShapeDtypeStruct((n, D_MODEL), F32),
        compiler_params=_params("parallel"),
        name="merge",
    )(h, o_attn, y_hyena, x0, wg, bg, wa, wh, wo, ln_g, ln_b)


ROUTE_TILE = 1024
ROW_TILE = 512
EXPERT_ROWS = 512
SLAB = 8
ISSUE_UNROLL = 8
assert SLAB * LANES == D_MODEL
L_E1, L_E2, L_G1, L_G2, L_R1, L_R2 = 0, 1, 2, 3, 4, 5


def _router_kernel(x_ref, w_ref, b_ref, route_ref, count_ref):
    logits = jnp.dot(x_ref[...], w_ref[...], precision=lax.Precision.HIGHEST,
                     preferred_element_type=F32) + b_ref[...]
    tm = logits.shape[0]
    lane_i = lax.broadcasted_iota(jnp.int32, logits.shape, 1)
    lane = lane_i.astype(F32)
    group_of_lane = jnp.right_shift(lane_i, 2).astype(F32)
    neg = -jnp.inf

    def first_argmax(vals, vmax):
        return jnp.min(jnp.where(vals == vmax, lane, float(LANES)), axis=-1, keepdims=True)

    gl = jnp.where((lane_i >= N_EXPERTS) & (lane_i < N_EXPERTS + N_GROUPS), logits, neg)
    gmax = jnp.max(gl, axis=-1, keepdims=True)
    p_group = 1.0 / jnp.sum(jnp.exp(gl - gmax), axis=-1, keepdims=True)
    g_sel = first_argmax(gl, gmax) - float(N_EXPERTS)
    el = jnp.where((lane_i < N_EXPERTS) & (group_of_lane == g_sel), logits, neg)
    m1 = jnp.max(el, axis=-1, keepdims=True)
    i1 = first_argmax(el, m1)
    el2 = jnp.where(lane == i1, neg, el)
    m2 = jnp.max(el2, axis=-1, keepdims=True)
    i2 = first_argmax(el2, m2)
    e2 = jnp.exp(m2 - m1)
    denom = 1.0 / (1.0 + e2)
    gate1 = denom * p_group
    gate2 = e2 * denom * p_group
    hit1 = lane == i1
    hit2 = lane == i2
    onehot = jnp.where(hit1 | hit2, 1.0, 0.0)
    earlier = (lax.broadcasted_iota(jnp.int32, (tm, tm), 0) > lax.broadcasted_iota(jnp.int32, (tm, tm), 1))
    before = jnp.dot(jnp.where(earlier, 1.0, 0.0).astype(BF16), onehot.astype(BF16), preferred_element_type=F32)
    rank1 = jnp.sum(jnp.where(hit1, before, 0.0), axis=-1, keepdims=True)
    rank2 = jnp.sum(jnp.where(hit2, before, 0.0), axis=-1, keepdims=True)
    out = jnp.zeros_like(logits)
    for l, val in ((L_E1, i1), (L_E2, i2), (L_G1, gate1), (L_G2, gate2), (L_R1, rank1), (L_R2, rank2)):
        out = jnp.where(lane_i == l, val, out)
    route_ref[...] = out
    count_ref[...] = jnp.broadcast_to(jnp.sum(onehot, axis=0, keepdims=True), count_ref.shape)


def _router(x, w, b):
    n = x.shape[0]
    tm = _row_tile(n, ROUTE_TILE)
    nt = n // tm
    return pl.pallas_call(
        _router_kernel,
        grid=(nt,),
        in_specs=[pl.BlockSpec((tm, D_MODEL), lambda i: (i, 0)),
                  pl.BlockSpec((D_MODEL, LANES), lambda i: (0, 0)),
                  pl.BlockSpec((1, LANES), lambda i: (0, 0))],
        out_specs=[pl.BlockSpec((tm, LANES), lambda i: (i, 0)),
                   pl.BlockSpec((None, 8, LANES), lambda i: (i, 0, 0))],
        out_shape=[jax.ShapeDtypeStruct((n, LANES), F32), jax.ShapeDtypeStruct((nt, 8, LANES), F32)],
        compiler_params=_params("parallel"),
        name="router",
    )(x, w, b)


def _routing_tables(route, counts, n):
    tm = _row_tile(n, ROUTE_TILE)
    nt = n // tm
    cnt = counts[:, 0, :N_EXPERTS].astype(jnp.int32)
    total = jnp.sum(cnt, axis=0)
    padded = (total + EXPERT_ROWS - 1) // EXPERT_ROWS * EXPERT_ROWS
    ends = jnp.cumsum(padded)
    tile_base = (ends - padded)[None, :] + jnp.cumsum(cnt, axis=0) - cnt
    base_tok = jnp.broadcast_to(tile_base[:, None, :], (nt, tm, N_EXPERTS)).reshape(n, N_EXPERTS)
    experts = jnp.arange(N_EXPERTS, dtype=jnp.int32)[None, :]

    def position(l_e, l_r):
        e = route[:, l_e].astype(jnp.int32)
        return jnp.sum(jnp.where(e[:, None] == experts, base_tok, 0), axis=1) + route[:, l_r].astype(jnp.int32)

    pos = jnp.stack([position(L_E1, L_R1), position(L_E2, L_R2)], axis=1).reshape(2 * n)
    n_blocks = 2 * n // EXPERT_ROWS + N_EXPERTS
    block_start = jnp.arange(n_blocks, dtype=jnp.int32) * EXPERT_ROWS
    block_expert = jnp.sum((ends[None, :] <= block_start[:, None]).astype(jnp.int32), axis=1)
    block_valid = (block_expert < N_EXPERTS).astype(jnp.int32)
    last_block = jnp.where(padded > total, ends // EXPERT_ROWS - 1, -1)
    tail = ends[-1] // EXPERT_ROWS + jnp.arange(N_EXPERTS, dtype=jnp.int32)
    zero_blocks = jnp.concatenate([last_block, jnp.where(tail < n_blocks, tail, -1)]).astype(jnp.int32)
    return pos, jnp.minimum(block_expert, N_EXPERTS - 1), block_valid, zero_blocks, n_blocks


def _to_slabs(slab_ref, base, x):
    rows = x.shape[0]
    for s in range(SLAB):
        slab_ref[pl.ds(base + s, rows, stride=SLAB), :] = x[:, s * LANES:(s + 1) * LANES]


def _from_slabs(slab_ref, base, rows):
    return jnp.concatenate([slab_ref[pl.ds(base + s, rows, stride=SLAB), :] for s in range(SLAB)], axis=-1)


def _dispatch_kernel(pos_ref, zb_ref, x_ref, xs_ref, slab_a, slab_b, sems):
    tm = x_ref.shape[0] // 2
    step = pl.program_id(0)

    def tile_wait(slab_ref, sem):
        for _ in range(2):
            pltpu.make_async_copy(slab_ref, xs_ref.at[pl.ds(0, tm * SLAB)], sem).wait()

    @pl.when(step == 0)
    def _():
        slab_a[...] = jnp.zeros_like(slab_a)
        block = slab_a.shape[0]
        for j in range(zb_ref.shape[0]):
            @pl.when(zb_ref[j] >= 0)
            def _():
                pltpu.make_async_copy(slab_a, xs_ref.at[pl.ds(zb_ref[j] * block, block)], sems.at[0]).start()
        for j in range(zb_ref.shape[0]):
            @pl.when(zb_ref[j] >= 0)
            def _():
                pltpu.make_async_copy(slab_a, xs_ref.at[pl.ds(0, block)], sems.at[0]).wait()

    for half, slab_ref in enumerate((slab_a, slab_b)):
        sem = sems.at[half]

        @pl.when(step > 0)
        def _():
            tile_wait(slab_ref, sem)

        _to_slabs(slab_ref, 0, x_ref[half * tm:(half + 1) * tm, :])

        def issue(t, carry):
            for k in range(2):
                p = pos_ref[2 * (half * tm + t) + k]
                pltpu.make_async_copy(slab_ref.at[pl.ds(t * SLAB, SLAB)], xs_ref.at[pl.ds(p * SLAB, SLAB)],
                                      sem).start(priority=k)
            return carry

        lax.fori_loop(0, tm, issue, 0, unroll=ISSUE_UNROLL)

    @pl.when(step == pl.num_programs(0) - 1)
    def _():
        tile_wait(slab_a, sems.at[0])
        tile_wait(slab_b, sems.at[1])


def _dispatch(x, pos, zero_blocks, n_rows):
    n = x.shape[0]
    tm = _row_tile(n, ROW_TILE)
    assert tm == EXPERT_ROWS
    assert n % (2 * tm) == 0
    return pl.pallas_call(
        _dispatch_kernel,
        grid=(n // (2 * tm),),
        in_specs=[pl.BlockSpec((4 * tm,), lambda i: (i,), memory_space=pltpu.SMEM),
                  pl.BlockSpec(zero_blocks.shape, lambda i: (0,), memory_space=pltpu.SMEM),
                  pl.BlockSpec((2 * tm, D_MODEL), lambda i: (i, 0))],
        out_specs=pl.BlockSpec(memory_space=pl.ANY),
        out_shape=jax.ShapeDtypeStruct((n_rows * SLAB, LANES), F32),
        scratch_shapes=[pltpu.VMEM((tm * SLAB, LANES), F32), pltpu.VMEM((tm * SLAB, LANES), F32),
                        pltpu.SemaphoreType.DMA((2,))],
        compiler_params=_params("arbitrary"),
        name="dispatch",
    )(pos, zero_blocks, x)


def _expert_ffn_kernel(be_ref, bv_ref, xs_ref, w1_ref, w3_ref, w2_ref, y_ref):
    rows = xs_ref.shape[0] // SLAB
    valid = bv_ref[pl.program_id(0)] > 0

    @pl.when(valid)
    def _():
        xb = _from_slabs(xs_ref, 0, rows).astype(BF16)
        a = jnp.dot(xb, w1_ref[...], preferred_element_type=F32)
        g = jnp.dot(xb, w3_ref[...], preferred_element_type=F32)
        hidden = (a * _sigmoid(a) * g).astype(BF16)
        _to_slabs(y_ref, 0, jnp.dot(hidden, w2_ref[...], preferred_element_type=F32))

    @pl.when(jnp.logical_not(valid))
    def _():
        y_ref[...] = jnp.zeros_like(y_ref)


def _expert_ffn(xs, block_expert, block_valid, n_blocks, w1, w3, w2):
    rows = EXPERT_ROWS * SLAB
    blk = lambda b, be, bv: (b, 0)
    wsel = lambda b, be, bv: (be[b], 0, 0)
    return pl.pallas_call(
        _expert_ffn_kernel,
        grid_spec=pltpu.PrefetchScalarGridSpec(
            num_scalar_prefetch=2,
            grid=(n_blocks,),
            in_specs=[pl.BlockSpec((rows, LANES), blk),
                      pl.BlockSpec((None, D_MODEL, D_EXPERT), wsel),
                      pl.BlockSpec((None, D_MODEL, D_EXPERT), wsel),
                      pl.BlockSpec((None, D_EXPERT, D_MODEL), wsel)],
            out_specs=pl.BlockSpec((rows, LANES), blk)),
        out_shape=jax.ShapeDtypeStruct(xs.shape, F32),
        compiler_params=_params("arbitrary"),
        name="expert_ffn",
    )(block_expert, block_valid, xs, w1, w3, w2)


def _combine_kernel(pos_ref, pos_next_ref, x_ref, route_ref, g_ref, b_ref, y_ref, o_ref, slab_a, slab_b, sems):
    tm = x_ref.shape[0] // 2
    step = pl.program_id(0)

    def fetch(tile_pos_ref, half, slab_ref, sem):
        def issue(t, carry):
            for k in range(2):
                p = tile_pos_ref[2 * (half * tm + t) + k]
                pltpu.make_async_copy(y_ref.at[pl.ds(p * SLAB, SLAB)],
                                      slab_ref.at[pl.ds((k * tm + t) * SLAB, SLAB)], sem).start(priority=k)
            return carry

        lax.fori_loop(0, tm, issue, 0, unroll=ISSUE_UNROLL)

    def finish(half, slab_ref, sem):
        for k in range(2):
            pltpu.make_async_copy(y_ref.at[pl.ds(0, tm * SLAB)],
                                  slab_ref.at[pl.ds(k * tm * SLAB, tm * SLAB)], sem).wait()
        rows = slice(half * tm, (half + 1) * tm)
        route = route_ref[rows, :]
        ffn = (route[:, L_G1:L_G1 + 1] * _from_slabs(slab_ref, 0, tm)
               + route[:, L_G2:L_G2 + 1] * _from_slabs(slab_ref, tm * SLAB, tm))
        o_ref[rows, :] = _layer_norm(ALPHA * x_ref[rows, :] + ffn, g_ref[...], b_ref[...])

    @pl.when(step == 0)
    def _():
        fetch(pos_ref, 0, slab_a, sems.at[0])

    fetch(pos_ref, 1, slab_b, sems.at[1])
    finish(0, slab_a, sems.at[0])

    @pl.when(step < pl.num_programs(0) - 1)
    def _():
        fetch(pos_next_ref, 0, slab_a, sems.at[0])

    finish(1, slab_b, sems.at[1])


def _combine(x, route, pos, y, ln_g, ln_b):
    n = x.shape[0]
    tm = _row_tile(n, ROW_TILE)
    assert n % (2 * tm) == 0
    steps = n // (2 * tm)
    row = lambda i: (i, 0)
    fixed = lambda i: (0, 0)
    return pl.pallas_call(
        _combine_kernel,
        grid=(steps,),
        in_specs=[pl.BlockSpec((4 * tm,), lambda i: (i,), memory_space=pltpu.SMEM),
                  pl.BlockSpec((4 * tm,), lambda i: (jnp.minimum(i + 1, steps - 1),), memory_space=pltpu.SMEM),
                  pl.BlockSpec((2 * tm, D_MODEL), row), pl.BlockSpec((2 * tm, LANES), row),
                  pl.BlockSpec((1, D_MODEL), fixed), pl.BlockSpec((1, D_MODEL), fixed),
                  pl.BlockSpec(memory_space=pl.ANY)],
        out_specs=pl.BlockSpec((2 * tm, D_MODEL), row),
        out_shape=jax.ShapeDtypeStruct((n, D_MODEL), F32),
        scratch_shapes=[pltpu.VMEM((2 * tm * SLAB, LANES), F32), pltpu.VMEM((2 * tm * SLAB, LANES), F32),
                        pltpu.SemaphoreType.DMA((2,))],
        compiler_params=_params("arbitrary"),
        name="combine",
    )(pos, pos, x, route, ln_g, ln_b, y)


def _moe(x, lay):
    n = x.shape[0]
    route, counts = _router(x, lay["w_route"], lay["b_route"])
    pos, block_expert, block_valid, zero_blocks, n_blocks = _routing_tables(route, counts, n)
    xs = _dispatch(x, pos, zero_blocks, n_blocks * EXPERT_ROWS)
    y = _expert_ffn(xs, block_expert, block_valid, n_blocks, lay["w1"], lay["w3"], lay["w2"])
    return _combine(x, route, pos, y, lay["ln2_g"], lay["ln2_b"])


def _prepare_layer(i, w_in, b_in, short_w, short_b, na_rpb, w_branch_attn, w_branch_hyena, w_out,
                   ln1_g, ln1_b, w_group, b_group, w_router, b_router, w1, w3, w2, ln2_g, ln2_b, flt_bias):
    row = lambda a: a[None, :].astype(F32)
    w_route = jnp.zeros((D_MODEL, LANES), F32)
    w_route = w_route.at[:, 0:N_EXPERTS].set(w_router[i]).at[:, N_EXPERTS:N_EXPERTS + N_GROUPS].set(w_group[i])
    b_route = jnp.zeros((1, LANES), F32)
    b_route = b_route.at[0, 0:N_EXPERTS].set(b_router[i]).at[0, N_EXPERTS:N_EXPERTS + N_GROUPS].set(b_group[i])
    return dict(
        w_qkvh=w_in[i][:, :D_QKVH].astype(BF16), b_qkvh=row(b_in[i][:D_QKVH]),
        w_gates=w_in[i][:, D_QKVH:].astype(BF16), b_gates=row(b_in[i][D_QKVH:]),
        short_w=short_w[i], short_b=row(short_b[i]), flt_bias=flt_bias[i],
        na_bias=_na_bias_table(na_rpb[i]),
        wa=w_branch_attn[i].astype(BF16), wh=w_branch_hyena[i].astype(BF16), wo=w_out[i].astype(BF16),
        ln1_g=row(ln1_g[i]), ln1_b=row(ln1_b[i]), ln2_g=row(ln2_g[i]), ln2_b=row(ln2_b[i]),
        w_route=w_route, b_route=b_route,
        w1=w1[i].astype(BF16), w3=w3[i].astype(BF16), w2=w2[i].astype(BF16),
    )


def _trunk(x, ln_in_g, ln_in_b, layers, filt):
    b, l, d = x.shape
    n = b * l
    plan = _DftPlan(l)
    z_feat = _filter_features(l)
    head_mask = jnp.asarray(_head_mask(), BF16)
    h = x.reshape(n, d)
    for i, lay in enumerate(layers):
        outs = _inproj(h, l, ln_in_g[None, :], ln_in_b[None, :], lay["w_qkvh"], lay["b_qkvh"],
                       lay["short_w"], lay["short_b"], apply_ln=(i == 0))
        if i == 0:
            h, q, k, v, u, x0 = outs
        else:
            q, k, v, u, x0 = outs
        seq = lambda a: a.reshape(b, l, a.shape[-1])
        o_attn = _neighbourhood_attention(seq(q), seq(k), seq(v), lay["na_bias"], head_mask).reshape(n, D_ATTN)
        kf = _hyena_filter_spectrum(l, plan, z_feat, *[f[i] for f in filt])
        y_hyena = _hyena_long_conv(seq(u), plan, kf, lay["flt_bias"])
        h = _merge(h, o_attn, y_hyena, x0, lay["w_gates"], lay["b_gates"], lay["wa"], lay["wh"], lay["wo"],
                   lay["ln1_g"], lay["ln1_b"])
        h = _moe(h, lay)
    return h.reshape(b, l, d)


def kernel(x_prompt, x_sample, ln_in_g, ln_in_b, w_in, b_in, short_w, short_b, na_rpb, flt_w1, flt_b1, flt_w2, flt_b2, flt_w3, flt_b3, flt_w4, flt_freq, flt_bias, w_branch_attn, w_branch_hyena, w_out, ln1_g, ln1_b, w_group, b_group, w_router, b_router, w1, w3, w2, ln2_g, ln2_b):
    layers = [_prepare_layer(i, w_in, b_in, short_w, short_b, na_rpb, w_branch_attn, w_branch_hyena, w_out,
                             ln1_g, ln1_b, w_group, b_group, w_router, b_router, w1, w3, w2, ln2_g, ln2_b,
                             flt_bias) for i in range(DEPTH)]
    filt = (flt_w1, flt_b1, flt_w2, flt_b2, flt_w3, flt_b3, flt_w4, flt_freq)
    y_prompt = _trunk(x_prompt, ln_in_g, ln_in_b, layers, filt)
    y_sample = _trunk(x_sample, ln_in_g, ln_in_b, layers, filt)
    return (y_prompt, y_sample)
```

```python
import functools
import math

import numpy as np
import jax
import jax.numpy as jnp
from jax import lax
from jax.experimental import pallas as pl
from jax.experimental.pallas import tpu as pltpu

F32 = jnp.float32
BF16 = jnp.bfloat16

D_MODEL = 1024
DEPTH = 2
GRID_W = 64
NA_HEADS = 8
NA_HEAD_DIM = 64
D_ATTN = NA_HEADS * NA_HEAD_DIM
NA_WIN_ROWS = 8
NA_WIN_COLS = 16
D_HYENA = D_MODEL // 2
FILTER_EMB = 33
FILTER_HIDDEN = 64
DECAY_FAST = 0.3
DECAY_SLOW = 1.5
DECAY_TARGET = 1e-2
N_GROUPS = 4
EXPERTS_PER_GROUP = 4
N_EXPERTS = N_GROUPS * EXPERTS_PER_GROUP
D_EXPERT = 512
LN_EPS = 1e-5
ALPHA = (2.0 * DEPTH) ** 0.25
D_QKVH = 3 * D_ATTN + 3 * D_HYENA
D_GATES = 2 * D_MODEL

LANES = 128
VMEM_LIMIT = 56 * 1024 * 1024
NEG_BIG = -1e30
LOG2E = math.log2(math.e)


def _params(*sem):
    return pltpu.CompilerParams(dimension_semantics=sem, vmem_limit_bytes=VMEM_LIMIT)


def _layer_norm(x, g, b):
    mu = jnp.mean(x, axis=-1, keepdims=True)
    xc = x - mu
    var = jnp.mean(xc * xc, axis=-1, keepdims=True)
    return xc * lax.rsqrt(var + LN_EPS) * g + b


def _row_tile(n, want):
    t = min(n, want)
    assert n % t == 0, (n, t)
    return t


HALO = 8


def _inproj_kernel(apply_ln, seq_tiles, x_ref, prev_ref, next_ref, g_ref, b_ref, w_ref, bias_ref,
                   sw_ref, sb_ref, *out_refs):
    x = x_ref[...]
    halo = jnp.concatenate([prev_ref[...], next_ref[...]], axis=0)
    if apply_ln:
        h_ref, q_ref, k_ref, v_ref, u_ref, x0_ref = out_refs
        x = _layer_norm(x, g_ref[...], b_ref[...])
        halo = _layer_norm(halo, g_ref[...], b_ref[...])
        h_ref[...] = x
    else:
        q_ref, k_ref, v_ref, u_ref, x0_ref = out_refs
    xb = x.astype(BF16)
    tm = xb.shape[0]

    def proj(rows, lo, hi):
        return jnp.dot(rows, w_ref[:, lo:hi], preferred_element_type=F32) + bias_ref[:, lo:hi]

    q_ref[...] = (proj(xb, 0, D_ATTN) * (NA_HEAD_DIM ** -0.5 * LOG2E)).astype(BF16)
    k_ref[...] = proj(xb, D_ATTN, 2 * D_ATTN).astype(BF16)
    v_ref[...] = proj(xb, 2 * D_ATTN, 3 * D_ATTN).astype(BF16)

    hy = proj(jnp.concatenate([xb, halo.astype(BF16)], axis=0), 3 * D_ATTN, D_QKVH)
    cur = hy[0:tm]
    i = pl.program_id(0) % seq_tiles
    prev_row = jnp.where(i > 0, hy[tm + HALO - 1:tm + HALO], 0.0)
    next_row = jnp.where(i < seq_tiles - 1, hy[tm + HALO:tm + HALO + 1], 0.0)
    row = lax.broadcasted_iota(jnp.int32, cur.shape, 0)
    up = jnp.where(row == 0, prev_row, pltpu.roll(cur, 1, axis=0))
    dn = jnp.where(row == tm - 1, next_row, pltpu.roll(cur, tm - 1, axis=0))
    c = sw_ref[0:1, :] * up + sw_ref[1:2, :] * cur + sw_ref[2:3, :] * dn + sb_ref[...]
    u_ref[...] = (c[:, 2 * D_HYENA:] * c[:, D_HYENA:2 * D_HYENA]).astype(BF16)
    x0_ref[...] = c[:, 0:D_HYENA].astype(BF16)


def _inproj(x, seq_len, ln_g, ln_b, w, bias, short_w, short_b, apply_ln):
    n = x.shape[0]
    tm = _row_tile(seq_len, 512)
    sub = tm // HALO
    row = lambda i: (i, 0)
    fixed = lambda i: (0, 0)
    prev = lambda i: (jnp.maximum(i * sub - 1, 0), 0)
    nxt = lambda i: (jnp.minimum((i + 1) * sub, n // HALO - 1), 0)
    out_shape = [jax.ShapeDtypeStruct((n, D_ATTN), BF16)] * 3 + [jax.ShapeDtypeStruct((n, D_HYENA), BF16)] * 2
    out_specs = [pl.BlockSpec((tm, D_ATTN), row)] * 3 + [pl.BlockSpec((tm, D_HYENA), row)] * 2
    if apply_ln:
        out_shape = [jax.ShapeDtypeStruct((n, D_MODEL), F32)] + out_shape
        out_specs = [pl.BlockSpec((tm, D_MODEL), row)] + out_specs
    return pl.pallas_call(
        functools.partial(_inproj_kernel, apply_ln, seq_len // tm),
        grid=(n // tm,),
        in_specs=[pl.BlockSpec((tm, D_MODEL), row), pl.BlockSpec((HALO, D_MODEL), prev),
                  pl.BlockSpec((HALO, D_MODEL), nxt), pl.BlockSpec((1, D_MODEL), fixed),
                  pl.BlockSpec((1, D_MODEL), fixed), pl.BlockSpec((D_MODEL, D_QKVH), fixed),
                  pl.BlockSpec((1, D_QKVH), fixed), pl.BlockSpec((3, 3 * D_HYENA), fixed),
                  pl.BlockSpec((1, 3 * D_HYENA), fixed)],
        out_specs=out_specs,
        out_shape=out_shape,
        compiler_params=_params("parallel"),
        name="inproj",
    )(x, x, x, ln_g, ln_b, w, bias, short_w, short_b)


def _na_bias_table(rpb):
    t = np.arange(GRID_W)
    kc = np.arange(GRID_W)
    col_start = np.clip(t - NA_WIN_COLS // 2, 0, GRID_W - NA_WIN_COLS)
    valid = (kc[None, :] >= col_start[:, None]) & (kc[None, :] < col_start[:, None] + NA_WIN_COLS)
    col_off = np.clip(kc[None, :] - t[:, None], 1 - NA_WIN_COLS, NA_WIN_COLS - 1) + (NA_WIN_COLS - 1)
    n_off = 2 * NA_WIN_COLS - 1
    pick = (col_off[:, :, None] == np.arange(n_off)[None, None, :]).astype(np.float32)
    g = jnp.einsum('hjs,tks->htjk', rpb.astype(F32), pick, precision=lax.Precision.HIGHEST)
    g = jnp.where(jnp.asarray(valid)[None, :, None, :], g * LOG2E, NEG_BIG)
    top = NA_WIN_ROWS - 1
    g = jnp.stack([g[:, :, top - c:top - c + NA_WIN_ROWS, :] for c in range(NA_WIN_ROWS)], axis=0)
    return g.reshape(NA_WIN_ROWS, NA_HEADS * GRID_W, NA_WIN_ROWS * GRID_W)


def _head_mask():
    r = np.arange(NA_HEADS * GRID_W)[:, None] // GRID_W
    c = np.arange(D_ATTN)[None, :] // NA_HEAD_DIM
    return (r == c).astype(np.float32)


NA_ROWS_PER_STEP = 4


def _na_row(rows, r, q, k_ref, v_ref, bias, hm):
    first = jnp.clip(r - NA_WIN_ROWS // 2, 0, rows - NA_WIN_ROWS)
    start = pl.multiple_of(first * GRID_W, GRID_W)
    nkeys = NA_WIN_ROWS * GRID_W
    kw = k_ref[pl.ds(start, nkeys), :]
    vw = v_ref[pl.ds(start, nkeys), :]
    pair_mask = hm[0:2 * GRID_W, 0:LANES]
    parts = []
    for j in range(D_ATTN // LANES):
        qj = q[:, j * LANES:(j + 1) * LANES]
        qbd = jnp.concatenate([qj, qj], axis=0) * pair_mask
        parts.append(lax.dot_general(qbd, kw[:, j * LANES:(j + 1) * LANES], (((1,), (1,)), ((), ())),
                                     preferred_element_type=F32))
    s = jnp.concatenate(parts, axis=0) + bias
    e = jnp.exp2(s - jnp.max(s, axis=-1, keepdims=True))
    inv = 1.0 / jnp.sum(e, axis=-1, keepdims=True)
    p = e.astype(BF16)
    low_half = lax.broadcasted_iota(jnp.int32, (GRID_W, LANES), 1) < NA_HEAD_DIM
    tiles = []
    for j in range(D_ATTN // LANES):
        pair = slice(2 * j * GRID_W, (2 * j + 2) * GRID_W)
        o = jnp.dot(p[pair], vw[:, j * LANES:(j + 1) * LANES], preferred_element_type=F32) * inv[pair]
        tiles.append(jnp.where(low_half, o[0:GRID_W], o[GRID_W:]))
    return jnp.concatenate(tiles, axis=-1)


def _na_kernel(rows, q_ref, k_ref, v_ref, *rest):
    bias_refs, (hm_ref, o_ref) = rest[:NA_ROWS_PER_STEP], rest[NA_ROWS_PER_STEP:]
    hm = hm_ref[...]
    for j in range(NA_ROWS_PER_STEP):
        r = pl.program_id(1) * NA_ROWS_PER_STEP + j
        tok = slice(j * GRID_W, (j + 1) * GRID_W)
        o_ref[tok, :] = _na_row(rows, r, q_ref[tok, :], k_ref, v_ref, bias_refs[j][...], hm).astype(BF16)


def _neighbourhood_attention(q, k, v, bias_table, head_mask):
    b, l, _ = q.shape
    rows = l // GRID_W
    assert rows >= NA_WIN_ROWS and l % GRID_W == 0 and rows % NA_ROWS_PER_STEP == 0
    assert 2 * NA_HEAD_DIM == LANES
    nkeys = NA_WIN_ROWS * GRID_W
    nq = NA_HEADS * GRID_W
    tq = NA_ROWS_PER_STEP * GRID_W

    def cls(j):
        def index(bi, i):
            r = i * NA_ROWS_PER_STEP + j
            return (r - jnp.clip(r - NA_WIN_ROWS // 2, 0, rows - NA_WIN_ROWS), 0, 0)
        return index

    return pl.pallas_call(
        functools.partial(_na_kernel, rows),
        grid=(b, rows // NA_ROWS_PER_STEP),
        in_specs=[pl.BlockSpec((None, tq, D_ATTN), lambda bi, i: (bi, i, 0)),
                  pl.BlockSpec((None, l, D_ATTN), lambda bi, i: (bi, 0, 0)),
                  pl.BlockSpec((None, l, D_ATTN), lambda bi, i: (bi, 0, 0))]
                 + [pl.BlockSpec((None, nq, nkeys), cls(j)) for j in range(NA_ROWS_PER_STEP)]
                 + [pl.BlockSpec((nq, D_ATTN), lambda bi, i: (0, 0))],
        out_specs=pl.BlockSpec((None, tq, D_ATTN), lambda bi, i: (bi, i, 0)),
        out_shape=jax.ShapeDtypeStruct((b, l, D_ATTN), BF16),
        compiler_params=_params("parallel", "arbitrary"),
        name="nattn",
    )(q, k, v, *([bias_table] * NA_ROWS_PER_STEP), head_mask)


def _filter_features(l):
    t = jnp.linspace(0.0, 1.0, l, dtype=F32)[:, None]
    bands = (FILTER_EMB - 1) // 2
    f = jnp.linspace(1e-4, bands - 1, bands, dtype=F32)[None, :]
    ang = f * (2.0 * math.pi / l) * jnp.arange(l, dtype=F32)[:, None]
    z = jnp.concatenate([t, jnp.cos(ang), -jnp.sin(ang)], axis=-1)
    z = jnp.concatenate([z, z[0:1], z[1:][::-1]], axis=0)
    return jnp.pad(z, ((0, 0), (0, LANES - FILTER_EMB)))


def _decay_rates():
    max_decay = math.log(DECAY_TARGET) / DECAY_FAST
    min_decay = math.log(DECAY_TARGET) / DECAY_SLOW
    return jnp.abs(jnp.linspace(min_decay, max_decay, D_HYENA, dtype=F32))[None, :]


def _filter_kernel(l, z_ref, w1_ref, b1_ref, w2_ref, b2_ref, w3_ref, b3_ref, w4_ref, freq_ref,
                   rate_ref, o_ref):
    hp = lax.Precision.HIGHEST
    z = z_ref[...]
    freq = freq_ref[...]
    h = jnp.sin(freq * (jnp.dot(z, w1_ref[...], precision=hp, preferred_element_type=F32) + b1_ref[...]))
    h = jnp.sin(freq * (jnp.dot(h, w2_ref[...], precision=hp, preferred_element_type=F32) + b2_ref[...]))
    h = jnp.sin(freq * (jnp.dot(h, w3_ref[...], precision=hp, preferred_element_type=F32) + b3_ref[...]))
    w4 = w4_ref[...]
    h_hi, w_hi = h.astype(BF16), w4.astype(BF16)
    h_lo, w_lo = (h - h_hi.astype(F32)).astype(BF16), (w4 - w_hi.astype(F32)).astype(BF16)
    h = (jnp.dot(h_hi, w_hi, preferred_element_type=F32) + jnp.dot(h_lo, w_hi, preferred_element_type=F32)
         + jnp.dot(h_hi, w_lo, preferred_element_type=F32))
    tr = z.shape[0]
    r = pl.program_id(0) * tr + lax.broadcasted_iota(jnp.int32, (tr, D_HYENA), 0)
    decay = jnp.exp(-z[:, 0:1] * rate_ref[...])
    tap = jnp.where(r < l, h[:, 0:D_HYENA], jnp.where(r > l, h[:, D_HYENA:], 0.0))
    o_ref[...] = (tap * decay * (0.5 / l)).astype(BF16)


def _filter_taps(l, z, w1, b1, w2, b2, w3, b3, w4, freq, rates):
    tr = _row_tile(2 * l, 512)
    fixed = lambda i: (0, 0)
    full = lambda a: pl.BlockSpec(a.shape, fixed)
    args = (w1, b1, w2, b2, w3, b3, w4, freq, rates)
    return pl.pallas_call(
        functools.partial(_filter_kernel, l),
        grid=(2 * l // tr,),
        in_specs=[pl.BlockSpec((tr, LANES), lambda i: (i, 0))] + [full(a) for a in args],
        out_specs=pl.BlockSpec((tr, D_HYENA), lambda i: (i, 0)),
        out_shape=jax.ShapeDtypeStruct((2 * l, D_HYENA), BF16),
        compiler_params=_params("parallel"),
        name="filter_taps",
    )(z, *args)


class _DftPlan:
    def __init__(self, l):
        n = 2 * l
        lg = n.bit_length() - 1
        assert 1 << lg == n
        self.n1 = n1 = 1 << (lg // 2)
        self.n2 = n2 = n // n1
        self.half = half = n1 // 2
        assert half % 16 == 0
        k1 = np.arange(n1)
        f1 = np.exp(-2j * np.pi * np.outer(k1, np.arange(n1)) / n1)
        f1h = f1[:, :half]
        self.a1_data = np.block([[f1h.real, -f1h.imag], [f1h.imag, f1h.real]]).astype(np.float32)
        self.a1_filt = np.concatenate([f1.real, f1.imag], axis=0).astype(np.float32)
        e = np.exp(2j * np.pi * np.outer(np.arange(half), k1) / n1)
        self.a3 = np.block([[e.real, -e.imag], [e.imag, e.real]]).astype(np.float32)
        tw = np.exp(-2j * np.pi * np.outer(k1, np.arange(n2)) / n)
        self.tw_r = tw.real.astype(np.float32)
        self.tw_i = tw.imag.astype(np.float32)
        f2 = np.exp(-2j * np.pi * np.outer(np.arange(n2), np.arange(n2)) / n2)
        self.f2_r = f2.real.astype(np.float32)
        self.f2_i = f2.imag.astype(np.float32)


N2_GROUP = 8
LANE_CHUNKS = D_HYENA // LANES
GROUP_COLS = N2_GROUP * D_HYENA


def _group_rows(ref):
    return ref.reshape(ref.shape[0] * ref.shape[1] * N2_GROUP, LANES)


def _load_group_row(ref, j):
    return _group_rows(ref)[pl.ds(j, ref.shape[0] * ref.shape[1], stride=N2_GROUP), :]


def _store_group_row(ref, j, val):
    _group_rows(ref)[pl.ds(j, ref.shape[0] * ref.shape[1], stride=N2_GROUP), :] = val


def _dft_rows_kernel(a_ref, x_ref, *o_refs):
    x = x_ref[...]
    x = x.reshape(x.shape[0] * x.shape[1], x.shape[2])
    res = jnp.dot(a_ref[...], x, preferred_element_type=F32)
    m = res.shape[0]
    for q, o_ref in enumerate(o_refs):
        for j in range(N2_GROUP):
            lo = j * D_HYENA + q * LANES
            _store_group_row(o_ref, j, res[:, lo:lo + LANES])


def _dft_rows(a, x, n2):
    p, _, half, cols = x.shape
    n1 = a.shape[0] // 2
    groups = n2 // N2_GROUP
    group_blk = pl.BlockSpec((None, 2, n1, None, N2_GROUP, LANES), lambda pi, g: (pi, 0, 0, g, 0, 0))
    outs = pl.pallas_call(
        _dft_rows_kernel,
        grid=(p, groups),
        in_specs=[pl.BlockSpec(a.shape, lambda pi, g: (0, 0)),
                  pl.BlockSpec((None, 2, half, GROUP_COLS), lambda pi, g: (pi, 0, 0, g))],
        out_specs=[group_blk] * LANE_CHUNKS,
        out_shape=[jax.ShapeDtypeStruct((p, 2, n1, groups, N2_GROUP, LANES), F32)] * LANE_CHUNKS,
        compiler_params=_params("parallel", "parallel"),
        name="dft_rows",
    )(a, x)
    return [o.reshape(p, 2, n1, n2, LANES) for o in outs]


def _twiddled_dft(f2r, f2i, twr, twi):
    gr = f2r * twr - f2i * twi
    gi = f2r * twi + f2i * twr
    return gr, gi


def _chunks(refs, lead):
    return jnp.concatenate([r[lead] for r in refs], axis=-1)


def _filter_spectrum_kernel(kb, *refs):
    z_refs = refs[:LANE_CHUNKS]
    f2r_ref, f2i_ref, twr_ref, twi_ref, o_ref = refs[LANE_CHUNKS:]
    f2r = f2r_ref[...]
    f2i = f2i_ref[...]

    def body(k, carry):
        gr, gi = _twiddled_dft(f2r, f2i, twr_ref[k], twi_ref[k])
        a_re = jnp.concatenate([gr, gi], axis=0).astype(BF16)
        a_im = jnp.concatenate([-gi, gr], axis=0).astype(BF16)
        y = (jnp.dot(a_re, _chunks(z_refs, (0, k)).astype(BF16), preferred_element_type=F32)
             + jnp.dot(a_im, _chunks(z_refs, (1, k)).astype(BF16), preferred_element_type=F32))
        n2 = f2r.shape[0]
        o_ref[0, k] = y[0:n2]
        o_ref[1, k] = y[n2:]
        return carry

    lax.fori_loop(0, kb, body, 0, unroll=True)


def _filter_spectrum(plan, z_chunks):
    _, _, n1, n2, _ = z_chunks[0].shape
    kb = min(n1, 8)
    fixed = lambda i: (0, 0)
    return pl.pallas_call(
        functools.partial(_filter_spectrum_kernel, kb),
        grid=(n1 // kb,),
        in_specs=[pl.BlockSpec((None, 2, kb, n2, LANES), lambda i: (0, 0, i, 0, 0))] * LANE_CHUNKS
                 + [pl.BlockSpec((n2, n2), fixed), pl.BlockSpec((n2, n2), fixed),
                    pl.BlockSpec((kb, 1, n2), lambda i: (i, 0, 0)), pl.BlockSpec((kb, 1, n2), lambda i: (i, 0, 0))],
        out_specs=pl.BlockSpec((2, kb, n2, D_HYENA), lambda i: (0, i, 0, 0)),
        out_shape=jax.ShapeDtypeStruct((2, n1, n2, D_HYENA), F32),
        compiler_params=_params("parallel"),
        name="filter_spectrum",
    )(*z_chunks, plan.f2_r, plan.f2_i, plan.tw_r[:, None, :], plan.tw_i[:, None, :])


def _spectral_kernel(kb, *refs):
    z_refs = refs[:LANE_CHUNKS]
    kf_ref, f2r_ref, f2i_ref, twr_ref, twi_ref, twrc_ref, twic_ref = refs[LANE_CHUNKS:LANE_CHUNKS + 7]
    o_refs = refs[LANE_CHUNKS + 7:]
    f2r = f2r_ref[...]
    f2i = f2i_ref[...]
    n2 = f2r.shape[0]

    def body(k, carry):
        gr, gi = _twiddled_dft(f2r, f2i, twr_ref[k], twi_ref[k])
        a_re = jnp.concatenate([gr, gi], axis=0).astype(BF16)
        a_im = jnp.concatenate([-gi, gr], axis=0).astype(BF16)
        y = (jnp.dot(a_re, _chunks(z_refs, (0, k)).astype(BF16), preferred_element_type=F32)
             + jnp.dot(a_im, _chunks(z_refs, (1, k)).astype(BF16), preferred_element_type=F32))
        yr, yi = y[0:n2], y[n2:]
        kr, ki = kf_ref[0, k], kf_ref[1, k]
        wr = (yr * kr - yi * ki).astype(BF16)
        wi = (yr * ki + yi * kr).astype(BF16)
        cr, ci = _twiddled_dft(f2r, f2i, twrc_ref[k], twic_ref[k])
        b_re = jnp.concatenate([cr, -ci], axis=0).astype(BF16)
        b_im = jnp.concatenate([ci, cr], axis=0).astype(BF16)
        v = (jnp.dot(b_re, wr, preferred_element_type=F32)
             + jnp.dot(b_im, wi, preferred_element_type=F32))
        for q, o_ref in enumerate(o_refs):
            o_ref[0, k] = v[0:n2, q * LANES:(q + 1) * LANES]
            o_ref[1, k] = v[n2:, q * LANES:(q + 1) * LANES]
        return carry

    lax.fori_loop(0, kb, body, 0, unroll=True)


def _spectral_multiply(plan, z_chunks, kf):
    p, _, n1, n2, _ = z_chunks[0].shape
    kb = min(n1, 8)
    zblk = pl.BlockSpec((None, 2, kb, n2, LANES), lambda i, pi: (pi, 0, i, 0, 0))
    fixed = lambda i, pi: (0, 0)
    trow = pl.BlockSpec((kb, 1, n2), lambda i, pi: (i, 0, 0))
    tcol = pl.BlockSpec((kb, n2, 1), lambda i, pi: (i, 0, 0))
    return pl.pallas_call(
        functools.partial(_spectral_kernel, kb),
        grid=(n1 // kb, p),
        in_specs=[zblk] * LANE_CHUNKS
                 + [pl.BlockSpec((2, kb, n2, D_HYENA), lambda i, pi: (0, i, 0, 0)),
                    pl.BlockSpec((n2, n2), fixed), pl.BlockSpec((n2, n2), fixed),
                    trow, trow, tcol, tcol],
        out_specs=[zblk] * LANE_CHUNKS,
        out_shape=[jax.ShapeDtypeStruct(z_chunks[0].shape, F32)] * LANE_CHUNKS,
        compiler_params=_params("parallel", "arbitrary"),
        name="spectral_multiply",
    )(*z_chunks, kf, plan.f2_r, plan.f2_i, plan.tw_r[:, None, :], plan.tw_i[:, None, :],
      plan.tw_r[:, :, None], plan.tw_i[:, :, None])


def _idft_rows_kernel(a_ref, *refs):
    v_refs = refs[:LANE_CHUNKS]
    u_ref, bias_ref, o_ref = refs[LANE_CHUNKS:]
    v = jnp.concatenate([_load_group_row(v_refs[q], j)
                         for j in range(N2_GROUP) for q in range(LANE_CHUNKS)], axis=-1)
    y = jnp.dot(a_ref[...], v.astype(BF16), preferred_element_type=F32)
    u = u_ref[...].astype(F32)
    o_ref[...] = (y.reshape(u.shape) + u * bias_ref[...]).astype(BF16)


def _idft_rows(a3, v_chunks, u, bias_cols):
    p, _, half, cols = u.shape
    _, _, n1, n2, _ = v_chunks[0].shape
    groups = n2 // N2_GROUP
    ublk = pl.BlockSpec((None, 2, half, GROUP_COLS), lambda pi, g: (pi, 0, 0, g))
    vblk = pl.BlockSpec((None, 2, n1, None, N2_GROUP, LANES), lambda pi, g: (pi, 0, 0, g, 0, 0))
    return pl.pallas_call(
        _idft_rows_kernel,
        grid=(p, groups),
        in_specs=[pl.BlockSpec(a3.shape, lambda pi, g: (0, 0))] + [vblk] * LANE_CHUNKS
                 + [ublk, pl.BlockSpec((1, GROUP_COLS), lambda pi, g: (0, 0))],
        out_specs=ublk,
        out_shape=jax.ShapeDtypeStruct(u.shape, BF16),
        compiler_params=_params("parallel", "parallel"),
        name="idft_rows",
    )(a3, *[v.reshape(p, 2, n1, groups, N2_GROUP, LANES) for v in v_chunks], u, bias_cols)


def _hyena_long_conv(u, plan, kf, flt_bias):
    b, l, _ = u.shape
    assert b % 2 == 0 and plan.n2 % N2_GROUP == 0
    pairs = u.reshape(b // 2, 2, plan.half, plan.n2 * D_HYENA)
    z = _dft_rows(jnp.asarray(plan.a1_data, BF16), pairs, plan.n2)
    v = _spectral_multiply(plan, z, kf)
    bias_cols = jnp.tile(flt_bias.astype(F32)[None, :], (1, N2_GROUP))
    y = _idft_rows(jnp.asarray(plan.a3, BF16), v, pairs, bias_cols)
    return y.reshape(b * l, D_HYENA)


def _hyena_filter_spectrum(l, plan, z_feat, w1, b1, w2, b2, w3, b3, w4, freq):
    w1p = jnp.pad(w1, ((0, LANES - FILTER_EMB), (0, 0)))
    taps = _filter_taps(l, z_feat, w1p, b1[None, :], w2, b2[None, :], w3, b3[None, :], w4,
                        freq[None, :], _decay_rates())
    z = _dft_rows(jnp.asarray(plan.a1_filt, BF16), taps.reshape(1, 2, plan.half, plan.n2 * D_HYENA), plan.n2)
    return _filter_spectrum(plan, z)


def _sigmoid(x):
    return 1.0 / (1.0 + jnp.exp(-x))


def _merge_kernel(h_ref, oa_ref, yh_ref, x0_ref, wg_ref, bg_ref, wa_ref, wh_ref, wo_ref, g_ref, b_ref, o_ref):
    h = h_ref[...]
    hb = h.astype(BF16)
    ga = jnp.dot(hb, wg_ref[:, 0:D_MODEL], preferred_element_type=F32) + bg_ref[:, 0:D_MODEL]
    gh = jnp.dot(hb, wg_ref[:, D_MODEL:], preferred_element_type=F32) + bg_ref[:, D_MODEL:]
    o_hyena = yh_ref[...] * x0_ref[...]
    merged = (_sigmoid(ga) * jnp.dot(oa_ref[...], wa_ref[...], preferred_element_type=F32)
              + _sigmoid(gh) * jnp.dot(o_hyena, wh_ref[...], preferred_element_type=F32))
    mix = jnp.dot(merged.astype(BF16), wo_ref[...], preferred_element_type=F32)
    o_ref[...] = _layer_norm(ALPHA * h + mix, g_ref[...], b_ref[...])


def _merge(h, o_attn, y_hyena, x0, wg, bg, wa, wh, wo, ln_g, ln_b):
    n = h.shape[0]
    tm = _row_tile(n, 512)
    row = lambda i: (i, 0)
    fixed = lambda i: (0, 0)
    full = lambda a: pl.BlockSpec(a.shape, fixed)
    return pl.pallas_call(
        _merge_kernel,
        grid=(n // tm,),
        in_specs=[pl.BlockSpec((tm, D_MODEL), row), pl.BlockSpec((tm, D_ATTN), row),
                  pl.BlockSpec((tm, D_HYENA), row), pl.BlockSpec((tm, D_HYENA), row)]
                 + [full(a) for a in (wg, bg, wa, wh, wo, ln_g, ln_b)],
        out_specs=pl.BlockSpec((tm, D_MODEL), row),
        out_shape=jax.ShapeDtypeStruct((n, D_MODEL), F32),
        compiler_params=_params("parallel"),
        name="merge",
    )(h, o_attn, y_hyena, x0, wg, bg, wa, wh, wo, ln_g, ln_b)


ROUTE_TILE = 1024
ROW_TILE = 512
EXPERT_ROWS = 512
SLAB = 8
ISSUE_UNROLL = 8
assert SLAB * LANES == D_MODEL
L_E1, L_E2, L_G1, L_G2, L_R1, L_R2 = 0, 1, 2, 3, 4, 5


def _router_kernel(x_ref, w_ref, b_ref, route_ref, count_ref):
    x, w = x_ref[...], w_ref[...]
    x_hi, w_hi = x.astype(BF16), w.astype(BF16)
    x_lo, w_lo = (x - x_hi.astype(F32)).astype(BF16), (w - w_hi.astype(F32)).astype(BF16)
    logits = (jnp.dot(x_hi, w_hi, preferred_element_type=F32) + jnp.dot(x_lo, w_hi, preferred_element_type=F32)
              + jnp.dot(x_hi, w_lo, preferred_element_type=F32)) + b_ref[...]
    tm = logits.shape[0]
    lane_i = lax.broadcasted_iota(jnp.int32, logits.shape, 1)
    lane = lane_i.astype(F32)
    group_of_lane = jnp.right_shift(lane_i, 2).astype(F32)
    neg = -jnp.inf

    def first_argmax(vals, vmax):
        return jnp.min(jnp.where(vals == vmax, lane, float(LANES)), axis=-1, keepdims=True)

    gl = jnp.where((lane_i >= N_EXPERTS) & (lane_i < N_EXPERTS + N_GROUPS), logits, neg)
    gmax = jnp.max(gl, axis=-1, keepdims=True)
    p_group = 1.0 / jnp.sum(jnp.exp(gl - gmax), axis=-1, keepdims=True)
    g_sel = first_argmax(gl, gmax) - float(N_EXPERTS)
    el = jnp.where((lane_i < N_EXPERTS) & (group_of_lane == g_sel), logits, neg)
    m1 = jnp.max(el, axis=-1, keepdims=True)
    i1 = first_argmax(el, m1)
    el2 = jnp.where(lane == i1, neg, el)
    m2 = jnp.max(el2, axis=-1, keepdims=True)
    i2 = first_argmax(el2, m2)
    e2 = jnp.exp(m2 - m1)
    denom = 1.0 / (1.0 + e2)
    gate1 = denom * p_group
    gate2 = e2 * denom * p_group
    hit1 = lane == i1
    hit2 = lane == i2
    onehot = jnp.where(hit1 | hit2, 1.0, 0.0)
    earlier = (lax.broadcasted_iota(jnp.int32, (tm, tm), 0) > lax.broadcasted_iota(jnp.int32, (tm, tm), 1))
    before = jnp.dot(jnp.where(earlier, 1.0, 0.0).astype(BF16), onehot.astype(BF16), preferred_element_type=F32)
    rank1 = jnp.sum(jnp.where(hit1, before, 0.0), axis=-1, keepdims=True)
    rank2 = jnp.sum(jnp.where(hit2, before, 0.0), axis=-1, keepdims=True)
    out = jnp.zeros_like(logits)
    for l, val in ((L_E1, i1), (L_E2, i2), (L_G1, gate1), (L_G2, gate2), (L_R1, rank1), (L_R2, rank2)):
        out = jnp.where(lane_i == l, val, out)
    route_ref[...] = out
    count_ref[...] = jnp.broadcast_to(jnp.sum(onehot, axis=0, keepdims=True), count_ref.shape)


def _router(x, w, b):
    n = x.shape[0]
    tm = _row_tile(n, ROUTE_TILE)
    nt = n // tm
    return pl.pallas_call(
        _router_kernel,
        grid=(nt,),
        in_specs=[pl.BlockSpec((tm, D_MODEL), lambda i: (i, 0)),
                  pl.BlockSpec((D_MODEL, LANES), lambda i: (0, 0)),
                  pl.BlockSpec((1, LANES), lambda i: (0, 0))],
        out_specs=[pl.BlockSpec((tm, LANES), lambda i: (i, 0)),
                   pl.BlockSpec((None, 8, LANES), lambda i: (i, 0, 0))],
        out_shape=[jax.ShapeDtypeStruct((n, LANES), F32), jax.ShapeDtypeStruct((nt, 8, LANES), F32)],
        compiler_params=_params("parallel"),
        name="router",
    )(x, w, b)


def _routing_tables(route, counts, n):
    tm = _row_tile(n, ROUTE_TILE)
    nt = n // tm
    cnt = counts[:, 0, :N_EXPERTS].astype(jnp.int32)
    total = jnp.sum(cnt, axis=0)
    padded = (total + EXPERT_ROWS - 1) // EXPERT_ROWS * EXPERT_ROWS
    ends = jnp.cumsum(padded)
    tile_base = (ends - padded)[None, :] + jnp.cumsum(cnt, axis=0) - cnt
    base_tok = jnp.broadcast_to(tile_base[:, None, :], (nt, tm, N_EXPERTS)).reshape(n, N_EXPERTS)
    experts = jnp.arange(N_EXPERTS, dtype=jnp.int32)[None, :]

    def position(l_e, l_r):
        e = route[:, l_e].astype(jnp.int32)
        return jnp.sum(jnp.where(e[:, None] == experts, base_tok, 0), axis=1) + route[:, l_r].astype(jnp.int32)

    pos = jnp.stack([position(L_E1, L_R1), position(L_E2, L_R2)], axis=1).reshape(2 * n)
    n_blocks = 2 * n // EXPERT_ROWS + N_EXPERTS
    block_start = jnp.arange(n_blocks, dtype=jnp.int32) * EXPERT_ROWS
    block_expert = jnp.sum((ends[None, :] <= block_start[:, None]).astype(jnp.int32), axis=1)
    block_valid = (block_expert < N_EXPERTS).astype(jnp.int32)
    last_block = jnp.where(padded > total, ends // EXPERT_ROWS - 1, -1)
    tail = ends[-1] // EXPERT_ROWS + jnp.arange(N_EXPERTS, dtype=jnp.int32)
    zero_blocks = jnp.concatenate([last_block, jnp.where(tail < n_blocks, tail, -1)]).astype(jnp.int32)
    return pos, jnp.minimum(block_expert, N_EXPERTS - 1), block_valid, zero_blocks, n_blocks


def _to_slabs(slab_ref, base, x):
    rows = x.shape[0]
    for s in range(SLAB):
        slab_ref[pl.ds(base + s, rows, stride=SLAB), :] = x[:, s * LANES:(s + 1) * LANES]


def _from_slabs(slab_ref, base, rows):
    return jnp.concatenate([slab_ref[pl.ds(base + s, rows, stride=SLAB), :] for s in range(SLAB)], axis=-1)


def _dispatch_kernel(pos_ref, zb_ref, x_ref, xs_ref, slab_a, slab_b, sems):
    tm = x_ref.shape[0] // 2
    step = pl.program_id(0)

    def tile_wait(slab_ref, sem):
        for _ in range(2):
            pltpu.make_async_copy(slab_ref, xs_ref.at[pl.ds(0, tm * SLAB)], sem).wait()

    @pl.when(step == 0)
    def _():
        slab_a[...] = jnp.zeros_like(slab_a)
        block = slab_a.shape[0]
        for j in range(zb_ref.shape[0]):
            @pl.when(zb_ref[j] >= 0)
            def _():
                pltpu.make_async_copy(slab_a, xs_ref.at[pl.ds(zb_ref[j] * block, block)], sems.at[0]).start()
        for j in range(zb_ref.shape[0]):
            @pl.when(zb_ref[j] >= 0)
            def _():
                pltpu.make_async_copy(slab_a, xs_ref.at[pl.ds(0, block)], sems.at[0]).wait()

    for half, slab_ref in enumerate((slab_a, slab_b)):
        sem = sems.at[half]

        @pl.when(step > 0)
        def _():
            tile_wait(slab_ref, sem)

        _to_slabs(slab_ref, 0, x_ref[half * tm:(half + 1) * tm, :])

        def issue(t, carry):
            for k in range(2):
                p = pos_ref[2 * (half * tm + t) + k]
                pltpu.make_async_copy(slab_ref.at[pl.ds(t * SLAB, SLAB)], xs_ref.at[pl.ds(p * SLAB, SLAB)],
                                      sem).start(priority=k)
            return carry

        lax.fori_loop(0, tm, issue, 0, unroll=ISSUE_UNROLL)

    @pl.when(step == pl.num_programs(0) - 1)
    def _():
        tile_wait(slab_a, sems.at[0])
        tile_wait(slab_b, sems.at[1])


def _dispatch(x, pos, zero_blocks, n_rows):
    n = x.shape[0]
    tm = _row_tile(n, ROW_TILE)
    assert tm == EXPERT_ROWS
    assert n % (2 * tm) == 0
    return pl.pallas_call(
        _dispatch_kernel,
        grid=(n // (2 * tm),),
        in_specs=[pl.BlockSpec((4 * tm,), lambda i: (i,), memory_space=pltpu.SMEM),
                  pl.BlockSpec(zero_blocks.shape, lambda i: (0,), memory_space=pltpu.SMEM),
                  pl.BlockSpec((2 * tm, D_MODEL), lambda i: (i, 0))],
        out_specs=pl.BlockSpec(memory_space=pl.ANY),
        out_shape=jax.ShapeDtypeStruct((n_rows * SLAB, LANES), F32),
        scratch_shapes=[pltpu.VMEM((tm * SLAB, LANES), F32), pltpu.VMEM((tm * SLAB, LANES), F32),
                        pltpu.SemaphoreType.DMA((2,))],
        compiler_params=_params("arbitrary"),
        name="dispatch",
    )(pos, zero_blocks, x)


def _expert_ffn_kernel(be_ref, bv_ref, xs_ref, w1_ref, w3_ref, w2_ref, y_ref):
    rows = xs_ref.shape[0] // SLAB
    valid = bv_ref[pl.program_id(0)] > 0

    @pl.when(valid)
    def _():
        xb = _from_slabs(xs_ref, 0, rows).astype(BF16)
        a = jnp.dot(xb, w1_ref[...], preferred_element_type=F32)
        g = jnp.dot(xb, w3_ref[...], preferred_element_type=F32)
        hidden = (a * _sigmoid(a) * g).astype(BF16)
        _to_slabs(y_ref, 0, jnp.dot(hidden, w2_ref[...], preferred_element_type=F32))

    @pl.when(jnp.logical_not(valid))
    def _():
        y_ref[...] = jnp.zeros_like(y_ref)


def _expert_ffn(xs, block_expert, block_valid, n_blocks, w1, w3, w2):
    rows = EXPERT_ROWS * SLAB
    blk = lambda b, be, bv: (b, 0)
    wsel = lambda b, be, bv: (be[b], 0, 0)
    return pl.pallas_call(
        _expert_ffn_kernel,
        grid_spec=pltpu.PrefetchScalarGridSpec(
            num_scalar_prefetch=2,
            grid=(n_blocks,),
            in_specs=[pl.BlockSpec((rows, LANES), blk),
                      pl.BlockSpec((None, D_MODEL, D_EXPERT), wsel),
                      pl.BlockSpec((None, D_MODEL, D_EXPERT), wsel),
                      pl.BlockSpec((None, D_EXPERT, D_MODEL), wsel)],
            out_specs=pl.BlockSpec((rows, LANES), blk)),
        out_shape=jax.ShapeDtypeStruct(xs.shape, F32),
        compiler_params=_params("arbitrary"),
        name="expert_ffn",
    )(block_expert, block_valid, xs, w1, w3, w2)


def _combine_kernel(pos_ref, pos_next_ref, x_ref, route_ref, g_ref, b_ref, y_ref, o_ref, slab_a, slab_b, sems):
    tm = x_ref.shape[0] // 2
    step = pl.program_id(0)

    def fetch(tile_pos_ref, half, slab_ref, sem):
        def issue(t, carry):
            for k in range(2):
                p = tile_pos_ref[2 * (half * tm + t) + k]
                pltpu.make_async_copy(y_ref.at[pl.ds(p * SLAB, SLAB)],
                                      slab_ref.at[pl.ds((k * tm + t) * SLAB, SLAB)], sem).start(priority=k)
            return carry

        lax.fori_loop(0, tm, issue, 0, unroll=ISSUE_UNROLL)

    def finish(half, slab_ref, sem):
        for k in range(2):
            pltpu.make_async_copy(y_ref.at[pl.ds(0, tm * SLAB)],
                                  slab_ref.at[pl.ds(k * tm * SLAB, tm * SLAB)], sem).wait()
        rows = slice(half * tm, (half + 1) * tm)
        route = route_ref[rows, :]
        ffn = (route[:, L_G1:L_G1 + 1] * _from_slabs(slab_ref, 0, tm)
               + route[:, L_G2:L_G2 + 1] * _from_slabs(slab_ref, tm * SLAB, tm))
        o_ref[rows, :] = _layer_norm(ALPHA * x_ref[rows, :] + ffn, g_ref[...], b_ref[...])

    @pl.when(step == 0)
    def _():
        fetch(pos_ref, 0, slab_a, sems.at[0])

    fetch(pos_ref, 1, slab_b, sems.at[1])
    finish(0, slab_a, sems.at[0])

    @pl.when(step < pl.num_programs(0) - 1)
    def _():
        fetch(pos_next_ref, 0, slab_a, sems.at[0])

    finish(1, slab_b, sems.at[1])


def _combine(x, route, pos, y, ln_g, ln_b):
    n = x.shape[0]
    tm = _row_tile(n, ROW_TILE)
    assert n % (2 * tm) == 0
    steps = n // (2 * tm)
    row = lambda i: (i, 0)
    fixed = lambda i: (0, 0)
    return pl.pallas_call(
        _combine_kernel,
        grid=(steps,),
        in_specs=[pl.BlockSpec((4 * tm,), lambda i: (i,), memory_space=pltpu.SMEM),
                  pl.BlockSpec((4 * tm,), lambda i: (jnp.minimum(i + 1, steps - 1),), memory_space=pltpu.SMEM),
                  pl.BlockSpec((2 * tm, D_MODEL), row), pl.BlockSpec((2 * tm, LANES), row),
                  pl.BlockSpec((1, D_MODEL), fixed), pl.BlockSpec((1, D_MODEL), fixed),
                  pl.BlockSpec(memory_space=pl.ANY)],
        out_specs=pl.BlockSpec((2 * tm, D_MODEL), row),
        out_shape=jax.ShapeDtypeStruct((n, D_MODEL), F32),
        scratch_shapes=[pltpu.VMEM((2 * tm * SLAB, LANES), F32), pltpu.VMEM((2 * tm * SLAB, LANES), F32),
                        pltpu.SemaphoreType.DMA((2,))],
        compiler_params=_params("arbitrary"),
        name="combine",
    )(pos, pos, x, route, ln_g, ln_b, y)


def _moe(x, lay):
    n = x.shape[0]
    route, counts = _router(x, lay["w_route"], lay["b_route"])
    pos, block_expert, block_valid, zero_blocks, n_blocks = _routing_tables(route, counts, n)
    xs = _dispatch(x, pos, zero_blocks, n_blocks * EXPERT_ROWS)
    y = _expert_ffn(xs, block_expert, block_valid, n_blocks, lay["w1"], lay["w3"], lay["w2"])
    return _combine(x, route, pos, y, lay["ln2_g"], lay["ln2_b"])


def _prepare_layer(i, w_in, b_in, short_w, short_b, na_rpb, w_branch_attn, w_branch_hyena, w_out,
                   ln1_g, ln1_b, w_group, b_group, w_router, b_router, w1, w3, w2, ln2_g, ln2_b, flt_bias):
    row = lambda a: a[None, :].astype(F32)
    w_route = jnp.zeros((D_MODEL, LANES), F32)
    w_route = w_route.at[:, 0:N_EXPERTS].set(w_router[i]).at[:, N_EXPERTS:N_EXPERTS + N_GROUPS].set(w_group[i])
    b_route = jnp.zeros((1, LANES), F32)
    b_route = b_route.at[0, 0:N_EXPERTS].set(b_router[i]).at[0, N_EXPERTS:N_EXPERTS + N_GROUPS].set(b_group[i])
    return dict(
        w_qkvh=w_in[i][:, :D_QKVH].astype(BF16), b_qkvh=row(b_in[i][:D_QKVH]),
        w_gates=w_in[i][:, D_QKVH:].astype(BF16), b_gates=row(b_in[i][D_QKVH:]),
        short_w=short_w[i], short_b=row(short_b[i]), flt_bias=flt_bias[i],
        na_bias=_na_bias_table(na_rpb[i]),
        wa=w_branch_attn[i].astype(BF16), wh=w_branch_hyena[i].astype(BF16), wo=w_out[i].astype(BF16),
        ln1_g=row(ln1_g[i]), ln1_b=row(ln1_b[i]), ln2_g=row(ln2_g[i]), ln2_b=row(ln2_b[i]),
        w_route=w_route, b_route=b_route,
        w1=w1[i].astype(BF16), w3=w3[i].astype(BF16), w2=w2[i].astype(BF16),
    )


def _trunk(x, ln_in_g, ln_in_b, layers, filt):
    b, l, d = x.shape
    n = b * l
    plan = _DftPlan(l)
    z_feat = _filter_features(l)
    head_mask = jnp.asarray(_head_mask(), BF16)
    h = x.reshape(n, d)
    for i, lay in enumerate(layers):
        outs = _inproj(h, l, ln_in_g[None, :], ln_in_b[None, :], lay["w_qkvh"], lay["b_qkvh"],
                       lay["short_w"], lay["short_b"], apply_ln=(i == 0))
        if i == 0:
            h, q, k, v, u, x0 = outs
        else:
            q, k, v, u, x0 = outs
        seq = lambda a: a.reshape(b, l, a.shape[-1])
        o_attn = _neighbourhood_attention(seq(q), seq(k), seq(v), lay["na_bias"], head_mask).reshape(n, D_ATTN)
        kf = _hyena_filter_spectrum(l, plan, z_feat, *[f[i] for f in filt])
        y_hyena = _hyena_long_conv(seq(u), plan, kf, lay["flt_bias"])
        h = _merge(h, o_attn, y_hyena, x0, lay["w_gates"], lay["b_gates"], lay["wa"], lay["wh"], lay["wo"],
                   lay["ln1_g"], lay["ln1_b"])
        h = _moe(h, lay)
    return h.reshape(b, l, d)


def kernel(x_prompt, x_sample, ln_in_g, ln_in_b, w_in, b_in, short_w, short_b, na_rpb, flt_w1, flt_b1, flt_w2, flt_b2, flt_w3, flt_b3, flt_w4, flt_freq, flt_bias, w_branch_attn, w_branch_hyena, w_out, ln1_g, ln1_b, w_group, b_group, w_router, b_router, w1, w3, w2, ln2_g, ln2_b):
    layers = [_prepare_layer(i, w_in, b_in, short_w, short_b, na_rpb, w_branch_attn, w_branch_hyena, w_out,
                             ln1_g, ln1_b, w_group, b_group, w_router, b_router, w1, w3, w2, ln2_g, ln2_b,
                             flt_bias) for i in range(DEPTH)]
    filt = (flt_w1, flt_b1, flt_w2, flt_b2, flt_w3, flt_b3, flt_w4, flt_freq)
    y_prompt = _trunk(x_prompt, ln_in_g, ln_in_b, layers, filt)
    y_sample = _trunk(x_sample, ln_in_g, ln_in_b, layers, filt)
    return (y_prompt, y_sample)
```

```python
import functools
import math

import numpy as np
import jax
import jax.numpy as jnp
from jax import lax
from jax.experimental import pallas as pl
from jax.experimental.pallas import tpu as pltpu

F32 = jnp.float32
BF16 = jnp.bfloat16

D_MODEL = 1024
DEPTH = 2
GRID_W = 64
NA_HEADS = 8
NA_HEAD_DIM = 64
D_ATTN = NA_HEADS * NA_HEAD_DIM
NA_WIN_ROWS = 8
NA_WIN_COLS = 16
D_HYENA = D_MODEL // 2
FILTER_EMB = 33
FILTER_HIDDEN = 64
DECAY_FAST = 0.3
DECAY_SLOW = 1.5
DECAY_TARGET = 1e-2
N_GROUPS = 4
EXPERTS_PER_GROUP = 4
N_EXPERTS = N_GROUPS * EXPERTS_PER_GROUP
D_EXPERT = 512
LN_EPS = 1e-5
ALPHA = (2.0 * DEPTH) ** 0.25
D_QKVH = 3 * D_ATTN + 3 * D_HYENA
D_GATES = 2 * D_MODEL

LANES = 128
VMEM_LIMIT = 56 * 1024 * 1024
NEG_BIG = -1e30
LOG2E = math.log2(math.e)


def _params(*sem):
    return pltpu.CompilerParams(dimension_semantics=sem, vmem_limit_bytes=VMEM_LIMIT)


def _layer_norm(x, g, b):
    mu = jnp.mean(x, axis=-1, keepdims=True)
    xc = x - mu
    var = jnp.mean(xc * xc, axis=-1, keepdims=True)
    return xc * lax.rsqrt(var + LN_EPS) * g + b


def _row_tile(n, want):
    t = min(n, want)
    assert n % t == 0, (n, t)
    return t


HALO = 8


def _inproj_kernel(apply_ln, seq_tiles, x_ref, prev_ref, next_ref, g_ref, b_ref, w_ref, bias_ref,
                   sw_ref, sb_ref, *out_refs):
    x = x_ref[...]
    halo = jnp.concatenate([prev_ref[...], next_ref[...]], axis=0)
    if apply_ln:
        h_ref, q_ref, k_ref, v_ref, u_ref, x0_ref = out_refs
        x = _layer_norm(x, g_ref[...], b_ref[...])
        halo = _layer_norm(halo, g_ref[...], b_ref[...])
        h_ref[...] = x
    else:
        q_ref, k_ref, v_ref, u_ref, x0_ref = out_refs
    xb = x.astype(BF16)
    tm = xb.shape[0]

    def proj(rows, lo, hi):
        return jnp.dot(rows, w_ref[:, lo:hi], preferred_element_type=F32) + bias_ref[:, lo:hi]

    q_ref[...] = (proj(xb, 0, D_ATTN) * (NA_HEAD_DIM ** -0.5 * LOG2E)).astype(BF16)
    k_ref[...] = proj(xb, D_ATTN, 2 * D_ATTN).astype(BF16)
    v_ref[...] = proj(xb, 2 * D_ATTN, 3 * D_ATTN).astype(BF16)

    hy = proj(jnp.concatenate([xb, halo.astype(BF16)], axis=0), 3 * D_ATTN, D_QKVH)
    cur = hy[0:tm]
    i = pl.program_id(0) % seq_tiles
    prev_row = jnp.where(i > 0, hy[tm + HALO - 1:tm + HALO], 0.0)
    next_row = jnp.where(i < seq_tiles - 1, hy[tm + HALO:tm + HALO + 1], 0.0)
    row = lax.broadcasted_iota(jnp.int32, cur.shape, 0)
    up = jnp.where(row == 0, prev_row, pltpu.roll(cur, 1, axis=0))
    dn = jnp.where(row == tm - 1, next_row, pltpu.roll(cur, tm - 1, axis=0))
    c = sw_ref[0:1, :] * up + sw_ref[1:2, :] * cur + sw_ref[2:3, :] * dn + sb_ref[...]
    u_ref[...] = c[:, 2 * D_HYENA:] * c[:, D_HYENA:2 * D_HYENA]
    x0_ref[...] = c[:, 0:D_HYENA].astype(BF16)


def _inproj(x, seq_len, ln_g, ln_b, w, bias, short_w, short_b, apply_ln):
    n = x.shape[0]
    tm = _row_tile(seq_len, 512)
    sub = tm // HALO
    row = lambda i: (i, 0)
    fixed = lambda i: (0, 0)
    prev = lambda i: (jnp.maximum(i * sub - 1, 0), 0)
    nxt = lambda i: (jnp.minimum((i + 1) * sub, n // HALO - 1), 0)
    out_shape = ([jax.ShapeDtypeStruct((n, D_ATTN), BF16)] * 3
                 + [jax.ShapeDtypeStruct((n, D_HYENA), F32), jax.ShapeDtypeStruct((n, D_HYENA), BF16)])
    out_specs = [pl.BlockSpec((tm, D_ATTN), row)] * 3 + [pl.BlockSpec((tm, D_HYENA), row)] * 2
    if apply_ln:
        out_shape = [jax.ShapeDtypeStruct((n, D_MODEL), F32)] + out_shape
        out_specs = [pl.BlockSpec((tm, D_MODEL), row)] + out_specs
    return pl.pallas_call(
        functools.partial(_inproj_kernel, apply_ln, seq_len // tm),
        grid=(n // tm,),
        in_specs=[pl.BlockSpec((tm, D_MODEL), row), pl.BlockSpec((HALO, D_MODEL), prev),
                  pl.BlockSpec((HALO, D_MODEL), nxt), pl.BlockSpec((1, D_MODEL), fixed),
                  pl.BlockSpec((1, D_MODEL), fixed), pl.BlockSpec((D_MODEL, D_QKVH), fixed),
                  pl.BlockSpec((1, D_QKVH), fixed), pl.BlockSpec((3, 3 * D_HYENA), fixed),
                  pl.BlockSpec((1, 3 * D_HYENA), fixed)],
        out_specs=out_specs,
        out_shape=out_shape,
        compiler_params=_params("parallel"),
        name="inproj",
    )(x, x, x, ln_g, ln_b, w, bias, short_w, short_b)


def _na_bias_table(rpb):
    t = np.arange(GRID_W)
    kc = np.arange(GRID_W)
    col_start = np.clip(t - NA_WIN_COLS // 2, 0, GRID_W - NA_WIN_COLS)
    valid = (kc[None, :] >= col_start[:, None]) & (kc[None, :] < col_start[:, None] + NA_WIN_COLS)
    col_off = np.clip(kc[None, :] - t[:, None], 1 - NA_WIN_COLS, NA_WIN_COLS - 1) + (NA_WIN_COLS - 1)
    n_off = 2 * NA_WIN_COLS - 1
    pick = (col_off[:, :, None] == np.arange(n_off)[None, None, :]).astype(np.float32)
    g = jnp.einsum('hjs,tks->htjk', rpb.astype(F32), pick, precision=lax.Precision.HIGHEST)
    g = jnp.where(jnp.asarray(valid)[None, :, None, :], g * LOG2E, NEG_BIG)
    top = NA_WIN_ROWS - 1
    g = jnp.stack([g[:, :, top - c:top - c + NA_WIN_ROWS, :] for c in range(NA_WIN_ROWS)], axis=0)
    return g.reshape(NA_WIN_ROWS, NA_HEADS * GRID_W, NA_WIN_ROWS * GRID_W)


def _head_mask():
    r = np.arange(NA_HEADS * GRID_W)[:, None] // GRID_W
    c = np.arange(D_ATTN)[None, :] // NA_HEAD_DIM
    return (r == c).astype(np.float32)


NA_ROWS_PER_STEP = 4


def _na_row(rows, r, q, k_ref, v_ref, bias, hm):
    first = jnp.clip(r - NA_WIN_ROWS // 2, 0, rows - NA_WIN_ROWS)
    start = pl.multiple_of(first * GRID_W, GRID_W)
    nkeys = NA_WIN_ROWS * GRID_W
    kw = k_ref[pl.ds(start, nkeys), :]
    vw = v_ref[pl.ds(start, nkeys), :]
    pair_mask = hm[0:2 * GRID_W, 0:LANES]
    parts = []
    for j in range(D_ATTN // LANES):
        qj = q[:, j * LANES:(j + 1) * LANES]
        qbd = jnp.concatenate([qj, qj], axis=0) * pair_mask
        parts.append(lax.dot_general(qbd, kw[:, j * LANES:(j + 1) * LANES], (((1,), (1,)), ((), ())),
                                     preferred_element_type=F32))
    s = jnp.concatenate(parts, axis=0) + bias
    e = jnp.exp2(s - jnp.max(s, axis=-1, keepdims=True))
    inv = 1.0 / jnp.sum(e, axis=-1, keepdims=True)
    p = e.astype(BF16)
    low_half = lax.broadcasted_iota(jnp.int32, (GRID_W, LANES), 1) < NA_HEAD_DIM
    tiles = []
    for j in range(D_ATTN // LANES):
        pair = slice(2 * j * GRID_W, (2 * j + 2) * GRID_W)
        o = jnp.dot(p[pair], vw[:, j * LANES:(j + 1) * LANES], preferred_element_type=F32) * inv[pair]
        tiles.append(jnp.where(low_half, o[0:GRID_W], o[GRID_W:]))
    return jnp.concatenate(tiles, axis=-1)


def _na_kernel(rows, q_ref, k_ref, v_ref, *rest):
    bias_refs, (hm_ref, o_ref) = rest[:NA_ROWS_PER_STEP], rest[NA_ROWS_PER_STEP:]
    hm = hm_ref[...]
    for j in range(NA_ROWS_PER_STEP):
        r = pl.program_id(1) * NA_ROWS_PER_STEP + j
        tok = slice(j * GRID_W, (j + 1) * GRID_W)
        o_ref[tok, :] = _na_row(rows, r, q_ref[tok, :], k_ref, v_ref, bias_refs[j][...], hm).astype(BF16)


def _neighbourhood_attention(q, k, v, bias_table, head_mask):
    b, l, _ = q.shape
    rows = l // GRID_W
    assert rows >= NA_WIN_ROWS and l % GRID_W == 0 and rows % NA_ROWS_PER_STEP == 0
    assert 2 * NA_HEAD_DIM == LANES
    nkeys = NA_WIN_ROWS * GRID_W
    nq = NA_HEADS * GRID_W
    tq = NA_ROWS_PER_STEP * GRID_W

    def cls(j):
        def index(bi, i):
            r = i * NA_ROWS_PER_STEP + j
            return (r - jnp.clip(r - NA_WIN_ROWS // 2, 0, rows - NA_WIN_ROWS), 0, 0)
        return index

    return pl.pallas_call(
        functools.partial(_na_kernel, rows),
        grid=(b, rows // NA_ROWS_PER_STEP),
        in_specs=[pl.BlockSpec((None, tq, D_ATTN), lambda bi, i: (bi, i, 0)),
                  pl.BlockSpec((None, l, D_ATTN), lambda bi, i: (bi, 0, 0)),
                  pl.BlockSpec((None, l, D_ATTN), lambda bi, i: (bi, 0, 0))]
                 + [pl.BlockSpec((None, nq, nkeys), cls(j)) for j in range(NA_ROWS_PER_STEP)]
                 + [pl.BlockSpec((nq, D_ATTN), lambda bi, i: (0, 0))],
        out_specs=pl.BlockSpec((None, tq, D_ATTN), lambda bi, i: (bi, i, 0)),
        out_shape=jax.ShapeDtypeStruct((b, l, D_ATTN), BF16),
        compiler_params=_params("parallel", "arbitrary"),
        name="nattn",
    )(q, k, v, *([bias_table] * NA_ROWS_PER_STEP), head_mask)


def _filter_features(l):
    t = jnp.linspace(0.0, 1.0, l, dtype=F32)[:, None]
    bands = (FILTER_EMB - 1) // 2
    f = jnp.linspace(1e-4, bands - 1, bands, dtype=F32)[None, :]
    ang = f * (2.0 * math.pi / l) * jnp.arange(l, dtype=F32)[:, None]
    z = jnp.concatenate([t, jnp.cos(ang), -jnp.sin(ang)], axis=-1)
    z = jnp.concatenate([z, z[0:1], z[1:][::-1]], axis=0)
    return jnp.pad(z, ((0, 0), (0, LANES - FILTER_EMB)))


def _decay_rates():
    max_decay = math.log(DECAY_TARGET) / DECAY_FAST
    min_decay = math.log(DECAY_TARGET) / DECAY_SLOW
    return jnp.abs(jnp.linspace(min_decay, max_decay, D_HYENA, dtype=F32))[None, :]


def _filter_kernel(l, z_ref, w1_ref, b1_ref, w2_ref, b2_ref, w3_ref, b3_ref, w4_ref, freq_ref,
                   rate_ref, o_ref):
    hp = lax.Precision.HIGHEST
    z = z_ref[...]
    freq = freq_ref[...]
    h = jnp.sin(freq * (jnp.dot(z, w1_ref[...], precision=hp, preferred_element_type=F32) + b1_ref[...]))
    h = jnp.sin(freq * (jnp.dot(h, w2_ref[...], precision=hp, preferred_element_type=F32) + b2_ref[...]))
    h = jnp.sin(freq * (jnp.dot(h, w3_ref[...], precision=hp, preferred_element_type=F32) + b3_ref[...]))
    w4 = w4_ref[...]
    h_hi, w_hi = h.astype(BF16), w4.astype(BF16)
    h_lo, w_lo = (h - h_hi.astype(F32)).astype(BF16), (w4 - w_hi.astype(F32)).astype(BF16)
    h = (jnp.dot(h_hi, w_hi, preferred_element_type=F32) + jnp.dot(h_lo, w_hi, preferred_element_type=F32)
         + jnp.dot(h_hi, w_lo, preferred_element_type=F32))
    tr = z.shape[0]
    r = pl.program_id(0) * tr + lax.broadcasted_iota(jnp.int32, (tr, D_HYENA), 0)
    decay = jnp.exp(-z[:, 0:1] * rate_ref[...])
    tap = jnp.where(r < l, h[:, 0:D_HYENA], jnp.where(r > l, h[:, D_HYENA:], 0.0))
    o_ref[...] = tap * decay * (0.5 / l)


def _filter_taps(l, z, w1, b1, w2, b2, w3, b3, w4, freq, rates):
    tr = _row_tile(2 * l, 512)
    fixed = lambda i: (0, 0)
    full = lambda a: pl.BlockSpec(a.shape, fixed)
    args = (w1, b1, w2, b2, w3, b3, w4, freq, rates)
    return pl.pallas_call(
        functools.partial(_filter_kernel, l),
        grid=(2 * l // tr,),
        in_specs=[pl.BlockSpec((tr, LANES), lambda i: (i, 0))] + [full(a) for a in args],
        out_specs=pl.BlockSpec((tr, D_HYENA), lambda i: (i, 0)),
        out_shape=jax.ShapeDtypeStruct((2 * l, D_HYENA), F32),
        compiler_params=_params("parallel"),
        name="filter_taps",
    )(z, *args)


class _DftPlan:
    def __init__(self, l):
        n = 2 * l
        lg = n.bit_length() - 1
        assert 1 << lg == n
        self.n1 = n1 = 1 << (lg // 2)
        self.n2 = n2 = n // n1
        self.half = half = n1 // 2
        assert half % 16 == 0
        k1 = np.arange(n1)
        f1 = np.exp(-2j * np.pi * np.outer(k1, np.arange(n1)) / n1)
        f1h = f1[:, :half]
        self.a1_data = np.block([[f1h.real, -f1h.imag], [f1h.imag, f1h.real]]).astype(np.float32)
        self.a1_filt = np.concatenate([f1.real, f1.imag], axis=0).astype(np.float32)
        e = np.exp(2j * np.pi * np.outer(np.arange(half), k1) / n1)
        self.a3 = np.block([[e.real, -e.imag], [e.imag, e.real]]).astype(np.float32)
        tw = np.exp(-2j * np.pi * np.outer(k1, np.arange(n2)) / n)
        self.tw_r = tw.real.astype(np.float32)
        self.tw_i = tw.imag.astype(np.float32)
        f2 = np.exp(-2j * np.pi * np.outer(np.arange(n2), np.arange(n2)) / n2)
        self.f2_r = f2.real.astype(np.float32)
        self.f2_i = f2.imag.astype(np.float32)


N2_GROUP = 8
LANE_CHUNKS = D_HYENA // LANES
GROUP_COLS = N2_GROUP * D_HYENA


def _group_rows(ref):
    return ref.reshape(ref.shape[0] * ref.shape[1] * N2_GROUP, LANES)


def _load_group_row(ref, j):
    return _group_rows(ref)[pl.ds(j, ref.shape[0] * ref.shape[1], stride=N2_GROUP), :]


def _store_group_row(ref, j, val):
    _group_rows(ref)[pl.ds(j, ref.shape[0] * ref.shape[1], stride=N2_GROUP), :] = val


def _gather_group(refs):
    return jnp.concatenate([_load_group_row(refs[q], j)
                            for j in range(N2_GROUP) for q in range(LANE_CHUNKS)], axis=-1)


def _dft_rows_kernel(a_ref, *refs):
    x_refs, o_refs = refs[:LANE_CHUNKS], refs[LANE_CHUNKS:]
    res = jnp.dot(a_ref[...], _gather_group(x_refs).astype(BF16), preferred_element_type=F32)
    for q, o_ref in enumerate(o_refs):
        for j in range(N2_GROUP):
            lo = j * D_HYENA + q * LANES
            _store_group_row(o_ref, j, res[:, lo:lo + LANES])


def _lane_chunk_specs(rows):
    def spec(q):
        return pl.BlockSpec((None, 2, rows, None, N2_GROUP, LANES), lambda pi, g: (pi, 0, 0, g, 0, q))
    return [spec(q) for q in range(LANE_CHUNKS)]


def _dft_rows(a, x, n2):
    p, _, half, _, _ = x.shape
    n1 = a.shape[0] // 2
    groups = n2 // N2_GROUP
    group_blk = pl.BlockSpec((None, 2, n1, None, N2_GROUP, LANES), lambda pi, g: (pi, 0, 0, g, 0, 0))
    x6 = x.reshape(p, 2, half, groups, N2_GROUP, D_HYENA)
    outs = pl.pallas_call(
        _dft_rows_kernel,
        grid=(p, groups),
        in_specs=[pl.BlockSpec(a.shape, lambda pi, g: (0, 0))] + _lane_chunk_specs(half),
        out_specs=[group_blk] * LANE_CHUNKS,
        out_shape=[jax.ShapeDtypeStruct((p, 2, n1, groups, N2_GROUP, LANES), F32)] * LANE_CHUNKS,
        compiler_params=_params("parallel", "parallel"),
        name="dft_rows",
    )(a, *([x6] * LANE_CHUNKS))
    return [o.reshape(p, 2, n1, n2, LANES) for o in outs]


def _twiddled_dft(f2r, f2i, twr, twi):
    gr = f2r * twr - f2i * twi
    gi = f2r * twi + f2i * twr
    return gr, gi


def _chunks(refs, lead):
    return jnp.concatenate([r[lead] for r in refs], axis=-1)


def _filter_spectrum_kernel(kb, *refs):
    z_refs = refs[:LANE_CHUNKS]
    f2r_ref, f2i_ref, twr_ref, twi_ref, o_ref = refs[LANE_CHUNKS:]
    f2r = f2r_ref[...]
    f2i = f2i_ref[...]

    def body(k, carry):
        gr, gi = _twiddled_dft(f2r, f2i, twr_ref[k], twi_ref[k])
        a_re = jnp.concatenate([gr, gi], axis=0).astype(BF16)
        a_im = jnp.concatenate([-gi, gr], axis=0).astype(BF16)
        y = (jnp.dot(a_re, _chunks(z_refs, (0, k)).astype(BF16), preferred_element_type=F32)
             + jnp.dot(a_im, _chunks(z_refs, (1, k)).astype(BF16), preferred_element_type=F32))
        n2 = f2r.shape[0]
        o_ref[0, k] = y[0:n2]
        o_ref[1, k] = y[n2:]
        return carry

    lax.fori_loop(0, kb, body, 0, unroll=True)


def _filter_spectrum(plan, z_chunks):
    _, _, n1, n2, _ = z_chunks[0].shape
    kb = min(n1, 8)
    fixed = lambda i: (0, 0)
    return pl.pallas_call(
        functools.partial(_filter_spectrum_kernel, kb),
        grid=(n1 // kb,),
        in_specs=[pl.BlockSpec((None, 2, kb, n2, LANES), lambda i: (0, 0, i, 0, 0))] * LANE_CHUNKS
                 + [pl.BlockSpec((n2, n2), fixed), pl.BlockSpec((n2, n2), fixed),
                    pl.BlockSpec((kb, 1, n2), lambda i: (i, 0, 0)), pl.BlockSpec((kb, 1, n2), lambda i: (i, 0, 0))],
        out_specs=pl.BlockSpec((2, kb, n2, D_HYENA), lambda i: (0, i, 0, 0)),
        out_shape=jax.ShapeDtypeStruct((2, n1, n2, D_HYENA), F32),
        compiler_params=_params("parallel"),
        name="filter_spectrum",
    )(*z_chunks, plan.f2_r, plan.f2_i, plan.tw_r[:, None, :], plan.tw_i[:, None, :])


def _spectral_kernel(kb, *refs):
    z_refs = refs[:LANE_CHUNKS]
    kf_ref, f2r_ref, f2i_ref, twr_ref, twi_ref, twrc_ref, twic_ref = refs[LANE_CHUNKS:LANE_CHUNKS + 7]
    o_refs = refs[LANE_CHUNKS + 7:]
    f2r = f2r_ref[...]
    f2i = f2i_ref[...]
    n2 = f2r.shape[0]

    def body(k, carry):
        gr, gi = _twiddled_dft(f2r, f2i, twr_ref[k], twi_ref[k])
        a_re = jnp.concatenate([gr, gi], axis=0).astype(BF16)
        a_im = jnp.concatenate([-gi, gr], axis=0).astype(BF16)
        y = (jnp.dot(a_re, _chunks(z_refs, (0, k)).astype(BF16), preferred_element_type=F32)
             + jnp.dot(a_im, _chunks(z_refs, (1, k)).astype(BF16), preferred_element_type=F32))
        yr, yi = y[0:n2], y[n2:]
        kr, ki = kf_ref[0, k], kf_ref[1, k]
        wr = (yr * kr - yi * ki).astype(BF16)
        wi = (yr * ki + yi * kr).astype(BF16)
        cr, ci = _twiddled_dft(f2r, f2i, twrc_ref[k], twic_ref[k])
        b_re = jnp.concatenate([cr, -ci], axis=0).astype(BF16)
        b_im = jnp.concatenate([ci, cr], axis=0).astype(BF16)
        v = (jnp.dot(b_re, wr, preferred_element_type=F32)
             + jnp.dot(b_im, wi, preferred_element_type=F32))
        for q, o_ref in enumerate(o_refs):
            o_ref[0, k] = v[0:n2, q * LANES:(q + 1) * LANES]
            o_ref[1, k] = v[n2:, q * LANES:(q + 1) * LANES]
        return carry

    lax.fori_loop(0, kb, body, 0, unroll=True)


def _spectral_multiply(plan, z_chunks, kf):
    p, _, n1, n2, _ = z_chunks[0].shape
    kb = min(n1, 8)
    zblk = pl.BlockSpec((None, 2, kb, n2, LANES), lambda i, pi: (pi, 0, i, 0, 0))
    fixed = lambda i, pi: (0, 0)
    trow = pl.BlockSpec((kb, 1, n2), lambda i, pi: (i, 0, 0))
    tcol = pl.BlockSpec((kb, n2, 1), lambda i, pi: (i, 0, 0))
    return pl.pallas_call(
        functools.partial(_spectral_kernel, kb),
        grid=(n1 // kb, p),
        in_specs=[zblk] * LANE_CHUNKS
                 + [pl.BlockSpec((2, kb, n2, D_HYENA), lambda i, pi: (0, i, 0, 0)),
                    pl.BlockSpec((n2, n2), fixed), pl.BlockSpec((n2, n2), fixed),
                    trow, trow, tcol, tcol],
        out_specs=[zblk] * LANE_CHUNKS,
        out_shape=[jax.ShapeDtypeStruct(z_chunks[0].shape, F32)] * LANE_CHUNKS,
        compiler_params=_params("parallel", "arbitrary"),
        name="spectral_multiply",
    )(*z_chunks, kf, plan.f2_r, plan.f2_i, plan.tw_r[:, None, :], plan.tw_i[:, None, :],
      plan.tw_r[:, :, None], plan.tw_i[:, :, None])


def _idft_rows_kernel(a_ref, *refs):
    v_refs, u_refs = refs[:LANE_CHUNKS], refs[LANE_CHUNKS:2 * LANE_CHUNKS]
    bias_ref = refs[2 * LANE_CHUNKS]
    o_refs = refs[2 * LANE_CHUNKS + 1:]
    y = jnp.dot(a_ref[...], _gather_group(v_refs).astype(BF16), preferred_element_type=F32)
    for q, o_ref in enumerate(o_refs):
        bias = bias_ref[:, q * LANES:(q + 1) * LANES]
        for j in range(N2_GROUP):
            lo = j * D_HYENA + q * LANES
            _store_group_row(o_ref, j, y[:, lo:lo + LANES] + _load_group_row(u_refs[q], j) * bias)


def _idft_rows(a3, v_chunks, u, bias):
    p, _, half, n2, _ = u.shape
    _, _, n1, _, _ = v_chunks[0].shape
    groups = n2 // N2_GROUP
    u6 = u.reshape(p, 2, half, groups, N2_GROUP, D_HYENA)
    vblk = pl.BlockSpec((None, 2, n1, None, N2_GROUP, LANES), lambda pi, g: (pi, 0, 0, g, 0, 0))
    yblk = pl.BlockSpec((None, 2, half, None, N2_GROUP, LANES), lambda pi, g: (pi, 0, 0, g, 0, 0))
    outs = pl.pallas_call(
        _idft_rows_kernel,
        grid=(p, groups),
        in_specs=[pl.BlockSpec(a3.shape, lambda pi, g: (0, 0))] + [vblk] * LANE_CHUNKS
                 + _lane_chunk_specs(half) + [pl.BlockSpec((1, D_HYENA), lambda pi, g: (0, 0))],
        out_specs=[yblk] * LANE_CHUNKS,
        out_shape=[jax.ShapeDtypeStruct((p, 2, half, groups, N2_GROUP, LANES), F32)] * LANE_CHUNKS,
        compiler_params=_params("parallel", "parallel"),
        name="idft_rows",
    )(a3, *[v.reshape(p, 2, n1, groups, N2_GROUP, LANES) for v in v_chunks], *([u6] * LANE_CHUNKS), bias)
    return [o.reshape(p * 2 * half * n2, LANES) for o in outs]


def _hyena_long_conv(u, plan, kf, flt_bias):
    b, l, _ = u.shape
    assert b % 2 == 0 and plan.n2 % N2_GROUP == 0
    pairs = u.reshape(b // 2, 2, plan.half, plan.n2, D_HYENA)
    z = _dft_rows(jnp.asarray(plan.a1_data, BF16), pairs, plan.n2)
    v = _spectral_multiply(plan, z, kf)
    return _idft_rows(jnp.asarray(plan.a3, BF16), v, pairs, flt_bias.astype(F32)[None, :])


def _hyena_filter_spectrum(l, plan, z_feat, w1, b1, w2, b2, w3, b3, w4, freq):
    w1p = jnp.pad(w1, ((0, LANES - FILTER_EMB), (0, 0)))
    taps = _filter_taps(l, z_feat, w1p, b1[None, :], w2, b2[None, :], w3, b3[None, :], w4,
                        freq[None, :], _decay_rates())
    z = _dft_rows(jnp.asarray(plan.a1_filt, BF16), taps.reshape(1, 2, plan.half, plan.n2, D_HYENA), plan.n2)
    return _filter_spectrum(plan, z)


def _sigmoid(x):
    return 1.0 / (1.0 + jnp.exp(-x))


def _merge_kernel(h_ref, oa_ref, *refs):
    yh_refs = refs[:LANE_CHUNKS]
    x0_ref, wg_ref, bg_ref, wa_ref, wh_ref, wo_ref, g_ref, b_ref, o_ref = refs[LANE_CHUNKS:]
    h = h_ref[...]
    hb = h.astype(BF16)
    ga = jnp.dot(hb, wg_ref[:, 0:D_MODEL], preferred_element_type=F32) + bg_ref[:, 0:D_MODEL]
    gh = jnp.dot(hb, wg_ref[:, D_MODEL:], preferred_element_type=F32) + bg_ref[:, D_MODEL:]
    y_hyena = jnp.concatenate([r[...] for r in yh_refs], axis=-1)
    o_hyena = (y_hyena * x0_ref[...].astype(F32)).astype(BF16)
    merged = (_sigmoid(ga) * jnp.dot(oa_ref[...], wa_ref[...], preferred_element_type=F32)
              + _sigmoid(gh) * jnp.dot(o_hyena, wh_ref[...], preferred_element_type=F32))
    mix = jnp.dot(merged.astype(BF16), wo_ref[...], preferred_element_type=F32)
    o_ref[...] = _layer_norm(ALPHA * h + mix, g_ref[...], b_ref[...])


def _merge(h, o_attn, y_hyena, x0, wg, bg, wa, wh, wo, ln_g, ln_b):
    n = h.shape[0]
    tm = _row_tile(n, 512)
    row = lambda i: (i, 0)
    fixed = lambda i: (0, 0)
    full = lambda a: pl.BlockSpec(a.shape, fixed)
    return pl.pallas_call(
        _merge_kernel,
        grid=(n // tm,),
        in_specs=[pl.BlockSpec((tm, D_MODEL), row), pl.BlockSpec((tm, D_ATTN), row)]
                 + [pl.BlockSpec((tm, LANES), row)] * LANE_CHUNKS + [pl.BlockSpec((tm, D_HYENA), row)]
                 + [full(a) for a in (wg, bg, wa, wh, wo, ln_g, ln_b)],
        out_specs=pl.BlockSpec((tm, D_MODEL), row),
        out_shape=jax.ShapeDtypeStruct((n, D_MODEL), F32),
        compiler_params=_params("parallel"),
        name="merge",
    )(h, o_attn, *y_hyena, x0, wg, bg, wa, wh, wo, ln_g, ln_b)


ROUTE_TILE = 1024
ROW_TILE = 512
EXPERT_ROWS = 512
SLAB = 8
ISSUE_UNROLL = 8
assert SLAB * LANES == D_MODEL
L_E1, L_E2, L_G1, L_G2, L_R1, L_R2 = 0, 1, 2, 3, 4, 5


def _router_kernel(x_ref, w_ref, b_ref, route_ref, count_ref):
    x, w = x_ref[...], w_ref[...]
    x_hi, w_hi = x.astype(BF16), w.astype(BF16)
    x_lo, w_lo = (x - x_hi.astype(F32)).astype(BF16), (w - w_hi.astype(F32)).astype(BF16)
    logits = (jnp.dot(x_hi, w_hi, preferred_element_type=F32) + jnp.dot(x_lo, w_hi, preferred_element_type=F32)
              + jnp.dot(x_hi, w_lo, preferred_element_type=F32)) + b_ref[...]
    tm = logits.shape[0]
    lane_i = lax.broadcasted_iota(jnp.int32, logits.shape, 1)
    lane = lane_i.astype(F32)
    group_of_lane = jnp.right_shift(lane_i, 2).astype(F32)
    neg = -jnp.inf

    def first_argmax(vals, vmax):
        return jnp.min(jnp.where(vals == vmax, lane, float(LANES)), axis=-1, keepdims=True)

    gl = jnp.where((lane_i >= N_EXPERTS) & (lane_i < N_EXPERTS + N_GROUPS), logits, neg)
    gmax = jnp.max(gl, axis=-1, keepdims=True)
    p_group = 1.0 / jnp.sum(jnp.exp(gl - gmax), axis=-1, keepdims=True)
    g_sel = first_argmax(gl, gmax) - float(N_EXPERTS)
    el = jnp.where((lane_i < N_EXPERTS) & (group_of_lane == g_sel), logits, neg)
    m1 = jnp.max(el, axis=-1, keepdims=True)
    i1 = first_argmax(el, m1)
    el2 = jnp.where(lane == i1, neg, el)
    m2 = jnp.max(el2, axis=-1, keepdims=True)
    i2 = first_argmax(el2, m2)
    e2 = jnp.exp(m2 - m1)
    denom = 1.0 / (1.0 + e2)
    gate1 = denom * p_group
    gate2 = e2 * denom * p_group
    hit1 = lane == i1
    hit2 = lane == i2
    onehot = jnp.where(hit1 | hit2, 1.0, 0.0)
    earlier = (lax.broadcasted_iota(jnp.int32, (tm, tm), 0) > lax.broadcasted_iota(jnp.int32, (tm, tm), 1))
    before = jnp.dot(jnp.where(earlier, 1.0, 0.0).astype(BF16), onehot.astype(BF16), preferred_element_type=F32)
    rank1 = jnp.sum(jnp.where(hit1, before, 0.0), axis=-1, keepdims=True)
    rank2 = jnp.sum(jnp.where(hit2, before, 0.0), axis=-1, keepdims=True)
    out = jnp.zeros_like(logits)
    for l, val in ((L_E1, i1), (L_E2, i2), (L_G1, gate1), (L_G2, gate2), (L_R1, rank1), (L_R2, rank2)):
        out = jnp.where(lane_i == l, val, out)
    route_ref[...] = out
    count_ref[...] = jnp.broadcast_to(jnp.sum(onehot, axis=0, keepdims=True), count_ref.shape)


def _router(x, w, b):
    n = x.shape[0]
    tm = _row_tile(n, ROUTE_TILE)
    nt = n // tm
    return pl.pallas_call(
        _router_kernel,
        grid=(nt,),
        in_specs=[pl.BlockSpec((tm, D_MODEL), lambda i: (i, 0)),
                  pl.BlockSpec((D_MODEL, LANES), lambda i: (0, 0)),
                  pl.BlockSpec((1, LANES), lambda i: (0, 0))],
        out_specs=[pl.BlockSpec((tm, LANES), lambda i: (i, 0)),
                   pl.BlockSpec((None, 8, LANES), lambda i: (i, 0, 0))],
        out_shape=[jax.ShapeDtypeStruct((n, LANES), F32), jax.ShapeDtypeStruct((nt, 8, LANES), F32)],
        compiler_params=_params("parallel"),
        name="router",
    )(x, w, b)


def _routing_tables(route, counts, n):
    tm = _row_tile(n, ROUTE_TILE)
    nt = n // tm
    cnt = counts[:, 0, :N_EXPERTS].astype(jnp.int32)
    total = jnp.sum(cnt, axis=0)
    padded = (total + EXPERT_ROWS - 1) // EXPERT_ROWS * EXPERT_ROWS
    ends = jnp.cumsum(padded)
    tile_base = (ends - padded)[None, :] + jnp.cumsum(cnt, axis=0) - cnt
    base_tok = jnp.broadcast_to(tile_base[:, None, :], (nt, tm, N_EXPERTS)).reshape(n, N_EXPERTS)
    experts = jnp.arange(N_EXPERTS, dtype=jnp.int32)[None, :]

    def position(l_e, l_r):
        e = route[:, l_e].astype(jnp.int32)
        return jnp.sum(jnp.where(e[:, None] == experts, base_tok, 0), axis=1) + route[:, l_r].astype(jnp.int32)

    pos = jnp.stack([position(L_E1, L_R1), position(L_E2, L_R2)], axis=1).reshape(2 * n)
    n_blocks = 2 * n // EXPERT_ROWS + N_EXPERTS
    block_start = jnp.arange(n_blocks, dtype=jnp.int32) * EXPERT_ROWS
    block_expert = jnp.sum((ends[None, :] <= block_start[:, None]).astype(jnp.int32), axis=1)
    block_valid = (block_expert < N_EXPERTS).astype(jnp.int32)
    last_block = jnp.where(padded > total, ends // EXPERT_ROWS - 1, -1)
    tail = ends[-1] // EXPERT_ROWS + jnp.arange(N_EXPERTS, dtype=jnp.int32)
    zero_blocks = jnp.concatenate([last_block, jnp.where(tail < n_blocks, tail, -1)]).astype(jnp.int32)
    return pos, jnp.minimum(block_expert, N_EXPERTS - 1), block_valid, zero_blocks, n_blocks


def _to_slabs(slab_ref, base, x):
    rows = x.shape[0]
    for s in range(SLAB):
        slab_ref[pl.ds(base + s, rows, stride=SLAB), :] = x[:, s * LANES:(s + 1) * LANES]


def _from_slabs(slab_ref, base, rows):
    return jnp.concatenate([slab_ref[pl.ds(base + s, rows, stride=SLAB), :] for s in range(SLAB)], axis=-1)


def _dispatch_kernel(pos_ref, zb_ref, x_ref, xs_ref, slab_a, slab_b, sems):
    tm = x_ref.shape[0] // 2
    step = pl.program_id(0)

    def tile_wait(slab_ref, sem):
        for _ in range(2):
            pltpu.make_async_copy(slab_ref, xs_ref.at[pl.ds(0, tm * SLAB)], sem).wait()

    @pl.when(step == 0)
    def _():
        slab_a[...] = jnp.zeros_like(slab_a)
        block = slab_a.shape[0]
        for j in range(zb_ref.shape[0]):
            @pl.when(zb_ref[j] >= 0)
            def _():
                pltpu.make_async_copy(slab_a, xs_ref.at[pl.ds(zb_ref[j] * block, block)], sems.at[0]).start()
        for j in range(zb_ref.shape[0]):
            @pl.when(zb_ref[j] >= 0)
            def _():
                pltpu.make_async_copy(slab_a, xs_ref.at[pl.ds(0, block)], sems.at[0]).wait()

    for half, slab_ref in enumerate((slab_a, slab_b)):
        sem = sems.at[half]

        @pl.when(step > 0)
        def _():
            tile_wait(slab_ref, sem)

        _to_slabs(slab_ref, 0, x_ref[half * tm:(half + 1) * tm, :])

        def issue(t, carry):
            for k in range(2):
                p = pos_ref[2 * (half * tm + t) + k]
                pltpu.make_async_copy(slab_ref.at[pl.ds(t * SLAB, SLAB)], xs_ref.at[pl.ds(p * SLAB, SLAB)],
                                      sem).start(priority=k)
            return carry

        lax.fori_loop(0, tm, issue, 0, unroll=ISSUE_UNROLL)

    @pl.when(step == pl.num_programs(0) - 1)
    def _():
        tile_wait(slab_a, sems.at[0])
        tile_wait(slab_b, sems.at[1])


def _dispatch(x, pos, zero_blocks, n_rows):
    n = x.shape[0]
    tm = _row_tile(n, ROW_TILE)
    assert tm == EXPERT_ROWS
    assert n % (2 * tm) == 0
    return pl.pallas_call(
        _dispatch_kernel,
        grid=(n // (2 * tm),),
        in_specs=[pl.BlockSpec((4 * tm,), lambda i: (i,), memory_space=pltpu.SMEM),
                  pl.BlockSpec(zero_blocks.shape, lambda i: (0,), memory_space=pltpu.SMEM),
                  pl.BlockSpec((2 * tm, D_MODEL), lambda i: (i, 0))],
        out_specs=pl.BlockSpec(memory_space=pl.ANY),
        out_shape=jax.ShapeDtypeStruct((n_rows * SLAB, LANES), F32),
        scratch_shapes=[pltpu.VMEM((tm * SLAB, LANES), F32), pltpu.VMEM((tm * SLAB, LANES), F32),
                        pltpu.SemaphoreType.DMA((2,))],
        compiler_params=_params("arbitrary"),
        name="dispatch",
    )(pos, zero_blocks, x)


def _expert_ffn_kernel(be_ref, bv_ref, xs_ref, w1_ref, w3_ref, w2_ref, y_ref):
    rows = xs_ref.shape[0] // SLAB
    valid = bv_ref[pl.program_id(0)] > 0

    @pl.when(valid)
    def _():
        xb = _from_slabs(xs_ref, 0, rows).astype(BF16)
        a = jnp.dot(xb, w1_ref[...], preferred_element_type=F32)
        g = jnp.dot(xb, w3_ref[...], preferred_element_type=F32)
        hidden = (a * _sigmoid(a) * g).astype(BF16)
        _to_slabs(y_ref, 0, jnp.dot(hidden, w2_ref[...], preferred_element_type=F32))

    @pl.when(jnp.logical_not(valid))
    def _():
        y_ref[...] = jnp.zeros_like(y_ref)


def _expert_ffn(xs, block_expert, block_valid, n_blocks, w1, w3, w2):
    rows = EXPERT_ROWS * SLAB
    blk = lambda b, be, bv: (b, 0)
    wsel = lambda b, be, bv: (be[b], 0, 0)
    return pl.pallas_call(
        _expert_ffn_kernel,
        grid_spec=pltpu.PrefetchScalarGridSpec(
            num_scalar_prefetch=2,
            grid=(n_blocks,),
            in_specs=[pl.BlockSpec((rows, LANES), blk),
                      pl.BlockSpec((None, D_MODEL, D_EXPERT), wsel),
                      pl.BlockSpec((None, D_MODEL, D_EXPERT), wsel),
                      pl.BlockSpec((None, D_EXPERT, D_MODEL), wsel)],
            out_specs=pl.BlockSpec((rows, LANES), blk)),
        out_shape=jax.ShapeDtypeStruct(xs.shape, F32),
        compiler_params=_params("arbitrary"),
        name="expert_ffn",
    )(block_expert, block_valid, xs, w1, w3, w2)


def _combine_kernel(pos_ref, pos_next_ref, x_ref, route_ref, g_ref, b_ref, y_ref, o_ref, slab_a, slab_b, sems):
    tm = x_ref.shape[0] // 2
    step = pl.program_id(0)

    def fetch(tile_pos_ref, half, slab_ref, sem):
        def issue(t, carry):
            for k in range(2):
                p = tile_pos_ref[2 * (half * tm + t) + k]
                pltpu.make_async_copy(y_ref.at[pl.ds(p * SLAB, SLAB)],
                                      slab_ref.at[pl.ds((k * tm + t) * SLAB, SLAB)], sem).start(priority=k)
            return carry

        lax.fori_loop(0, tm, issue, 0, unroll=ISSUE_UNROLL)

    def finish(half, slab_ref, sem):
        for k in range(2):
            pltpu.make_async_copy(y_ref.at[pl.ds(0, tm * SLAB)],
                                  slab_ref.at[pl.ds(k * tm * SLAB, tm * SLAB)], sem).wait()
        rows = slice(half * tm, (half + 1) * tm)
        route = route_ref[rows, :]
        ffn = (route[:, L_G1:L_G1 + 1] * _from_slabs(slab_ref, 0, tm)
               + route[:, L_G2:L_G2 + 1] * _from_slabs(slab_ref, tm * SLAB, tm))
        o_ref[rows, :] = _layer_norm(ALPHA * x_ref[rows, :] + ffn, g_ref[...], b_ref[...])

    @pl.when(step == 0)
    def _():
        fetch(pos_ref, 0, slab_a, sems.at[0])

    fetch(pos_ref, 1, slab_b, sems.at[1])
    finish(0, slab_a, sems.at[0])

    @pl.when(step < pl.num_programs(0) - 1)
    def _():
        fetch(pos_next_ref, 0, slab_a, sems.at[0])

    finish(1, slab_b, sems.at[1])


def _combine(x, route, pos, y, ln_g, ln_b):
    n = x.shape[0]
    tm = _row_tile(n, ROW_TILE)
    assert n % (2 * tm) == 0
    steps = n // (2 * tm)
    row = lambda i: (i, 0)
    fixed = lambda i: (0, 0)
    return pl.pallas_call(
        _combine_kernel,
        grid=(steps,),
        in_specs=[pl.BlockSpec((4 * tm,), lambda i: (i,), memory_space=pltpu.SMEM),
                  pl.BlockSpec((4 * tm,), lambda i: (jnp.minimum(i + 1, steps - 1),), memory_space=pltpu.SMEM),
                  pl.BlockSpec((2 * tm, D_MODEL), row), pl.BlockSpec((2 * tm, LANES), row),
                  pl.BlockSpec((1, D_MODEL), fixed), pl.BlockSpec((1, D_MODEL), fixed),
                  pl.BlockSpec(memory_space=pl.ANY)],
        out_specs=pl.BlockSpec((2 * tm, D_MODEL), row),
        out_shape=jax.ShapeDtypeStruct((n, D_MODEL), F32),
        scratch_shapes=[pltpu.VMEM((2 * tm * SLAB, LANES), F32), pltpu.VMEM((2 * tm * SLAB, LANES), F32),
                        pltpu.SemaphoreType.DMA((2,))],
        compiler_params=_params("arbitrary"),
        name="combine",
    )(pos, pos, x, route, ln_g, ln_b, y)


def _moe(x, lay):
    n = x.shape[0]
    route, counts = _router(x, lay["w_route"], lay["b_route"])
    pos, block_expert, block_valid, zero_blocks, n_blocks = _routing_tables(route, counts, n)
    xs = _dispatch(x, pos, zero_blocks, n_blocks * EXPERT_ROWS)
    y = _expert_ffn(xs, block_expert, block_valid, n_blocks, lay["w1"], lay["w3"], lay["w2"])
    return _combine(x, route, pos, y, lay["ln2_g"], lay["ln2_b"])


def _prepare_layer(i, w_in, b_in, short_w, short_b, na_rpb, w_branch_attn, w_branch_hyena, w_out,
                   ln1_g, ln1_b, w_group, b_group, w_router, b_router, w1, w3, w2, ln2_g, ln2_b, flt_bias):
    row = lambda a: a[None, :].astype(F32)
    w_route = jnp.zeros((D_MODEL, LANES), F32)
    w_route = w_route.at[:, 0:N_EXPERTS].set(w_router[i]).at[:, N_EXPERTS:N_EXPERTS + N_GROUPS].set(w_group[i])
    b_route = jnp.zeros((1, LANES), F32)
    b_route = b_route.at[0, 0:N_EXPERTS].set(b_router[i]).at[0, N_EXPERTS:N_EXPERTS + N_GROUPS].set(b_group[i])
    return dict(
        w_qkvh=w_in[i][:, :D_QKVH].astype(BF16), b_qkvh=row(b_in[i][:D_QKVH]),
        w_gates=w_in[i][:, D_QKVH:].astype(BF16), b_gates=row(b_in[i][D_QKVH:]),
        short_w=short_w[i], short_b=row(short_b[i]), flt_bias=flt_bias[i],
        na_bias=_na_bias_table(na_rpb[i]),
        wa=w_branch_attn[i].astype(BF16), wh=w_branch_hyena[i].astype(BF16), wo=w_out[i].astype(BF16),
        ln1_g=row(ln1_g[i]), ln1_b=row(ln1_b[i]), ln2_g=row(ln2_g[i]), ln2_b=row(ln2_b[i]),
        w_route=w_route, b_route=b_route,
        w1=w1[i].astype(BF16), w3=w3[i].astype(BF16), w2=w2[i].astype(BF16),
    )


def _trunk(x, ln_in_g, ln_in_b, layers, filt):
    b, l, d = x.shape
    n = b * l
    plan = _DftPlan(l)
    z_feat = _filter_features(l)
    head_mask = jnp.asarray(_head_mask(), BF16)
    h = x.reshape(n, d)
    for i, lay in enumerate(layers):
        outs = _inproj(h, l, ln_in_g[None, :], ln_in_b[None, :], lay["w_qkvh"], lay["b_qkvh"],
                       lay["short_w"], lay["short_b"], apply_ln=(i == 0))
        if i == 0:
            h, q, k, v, u, x0 = outs
        else:
            q, k, v, u, x0 = outs
        seq = lambda a: a.reshape(b, l, a.shape[-1])
        o_attn = _neighbourhood_attention(seq(q), seq(k), seq(v), lay["na_bias"], head_mask).reshape(n, D_ATTN)
        kf = _hyena_filter_spectrum(l, plan, z_feat, *[f[i] for f in filt])
        y_hyena = _hyena_long_conv(seq(u), plan, kf, lay["flt_bias"])
        h = _merge(h, o_attn, y_hyena, x0, lay["w_gates"], lay["b_gates"], lay["wa"], lay["wh"], lay["wo"],
                   lay["ln1_g"], lay["ln1_b"])
        h = _moe(h, lay)
    return h.reshape(b, l, d)


def kernel(x_prompt, x_sample, ln_in_g, ln_in_b, w_in, b_in, short_w, short_b, na_rpb, flt_w1, flt_b1, flt_w2, flt_b2, flt_w3, flt_b3, flt_w4, flt_freq, flt_bias, w_branch_attn, w_branch_hyena, w_out, ln1_g, ln1_b, w_group, b_group, w_router, b_router, w1, w3, w2, ln2_g, ln2_b):
    layers = [_prepare_layer(i, w_in, b_in, short_w, short_b, na_rpb, w_branch_attn, w_branch_hyena, w_out,
                             ln1_g, ln1_b, w_group, b_group, w_router, b_router, w1, w3, w2, ln2_g, ln2_b,
                             flt_bias) for i in range(DEPTH)]
    filt = (flt_w1, flt_b1, flt_w2, flt_b2, flt_w3, flt_b3, flt_w4, flt_freq)
    y_prompt = _trunk(x_prompt, ln_in_g, ln_in_b, layers, filt)
    y_sample = _trunk(x_sample, ln_in_g, ln_in_b, layers, filt)
    return (y_prompt, y_sample)
```

```python
import functools
import math

import numpy as np
import jax
import jax.numpy as jnp
from jax import lax
from jax.experimental import pallas as pl
from jax.experimental.pallas import tpu as pltpu

F32 = jnp.float32
BF16 = jnp.bfloat16

D_MODEL = 1024
DEPTH = 2
GRID_W = 64
NA_HEADS = 8
NA_HEAD_DIM = 64
D_ATTN = NA_HEADS * NA_HEAD_DIM
NA_WIN_ROWS = 8
NA_WIN_COLS = 16
D_HYENA = D_MODEL // 2
FILTER_EMB = 33
FILTER_HIDDEN = 64
DECAY_FAST = 0.3
DECAY_SLOW = 1.5
DECAY_TARGET = 1e-2
N_GROUPS = 4
EXPERTS_PER_GROUP = 4
N_EXPERTS = N_GROUPS * EXPERTS_PER_GROUP
D_EXPERT = 512
LN_EPS = 1e-5
ALPHA = (2.0 * DEPTH) ** 0.25
D_QKVH = 3 * D_ATTN + 3 * D_HYENA
D_GATES = 2 * D_MODEL

LANES = 128
VMEM_LIMIT = 56 * 1024 * 1024
NEG_BIG = -1e30
LOG2E = math.log2(math.e)


def _params(*sem):
    return pltpu.CompilerParams(dimension_semantics=sem, vmem_limit_bytes=VMEM_LIMIT)


def _layer_norm(x, g, b):
    mu = jnp.mean(x, axis=-1, keepdims=True)
    xc = x - mu
    var = jnp.mean(xc * xc, axis=-1, keepdims=True)
    return xc * lax.rsqrt(var + LN_EPS) * g + b


def _row_tile(n, want):
    t = min(n, want)
    assert n % t == 0, (n, t)
    return t


HALO = 8


def _inproj_kernel(apply_ln, seq_tiles, x_ref, prev_ref, next_ref, g_ref, b_ref, w_ref, bias_ref,
                   sw_ref, sb_ref, *out_refs):
    x = x_ref[...]
    halo = jnp.concatenate([prev_ref[...], next_ref[...]], axis=0)
    if apply_ln:
        h_ref, q_ref, k_ref, v_ref, u_ref, x0_ref = out_refs
        x = _layer_norm(x, g_ref[...], b_ref[...])
        halo = _layer_norm(halo, g_ref[...], b_ref[...])
        h_ref[...] = x
    else:
        q_ref, k_ref, v_ref, u_ref, x0_ref = out_refs
    xb = x.astype(BF16)
    tm = xb.shape[0]

    def proj(rows, lo, hi):
        return jnp.dot(rows, w_ref[:, lo:hi], preferred_element_type=F32) + bias_ref[:, lo:hi]

    q_ref[...] = (proj(xb, 0, D_ATTN) * (NA_HEAD_DIM ** -0.5 * LOG2E)).astype(BF16)
    k_ref[...] = proj(xb, D_ATTN, 2 * D_ATTN).astype(BF16)
    v_ref[...] = proj(xb, 2 * D_ATTN, 3 * D_ATTN).astype(BF16)

    hy = proj(jnp.concatenate([xb, halo.astype(BF16)], axis=0), 3 * D_ATTN, D_QKVH)
    cur = hy[0:tm]
    i = pl.program_id(0) % seq_tiles
    prev_row = jnp.where(i > 0, hy[tm + HALO - 1:tm + HALO], 0.0)
    next_row = jnp.where(i < seq_tiles - 1, hy[tm + HALO:tm + HALO + 1], 0.0)
    row = lax.broadcasted_iota(jnp.int32, cur.shape, 0)
    up = jnp.where(row == 0, prev_row, pltpu.roll(cur, 1, axis=0))
    dn = jnp.where(row == tm - 1, next_row, pltpu.roll(cur, tm - 1, axis=0))
    c = sw_ref[0:1, :] * up + sw_ref[1:2, :] * cur + sw_ref[2:3, :] * dn + sb_ref[...]
    u_ref[...] = c[:, 2 * D_HYENA:] * c[:, D_HYENA:2 * D_HYENA]
    x0_ref[...] = c[:, 0:D_HYENA].astype(BF16)


def _inproj(x, seq_len, ln_g, ln_b, w, bias, short_w, short_b, apply_ln):
    n = x.shape[0]
    tm = _row_tile(seq_len, 512)
    sub = tm // HALO
    row = lambda i: (i, 0)
    fixed = lambda i: (0, 0)
    prev = lambda i: (jnp.maximum(i * sub - 1, 0), 0)
    nxt = lambda i: (jnp.minimum((i + 1) * sub, n // HALO - 1), 0)
    out_shape = ([jax.ShapeDtypeStruct((n, D_ATTN), BF16)] * 3
                 + [jax.ShapeDtypeStruct((n, D_HYENA), F32), jax.ShapeDtypeStruct((n, D_HYENA), BF16)])
    out_specs = [pl.BlockSpec((tm, D_ATTN), row)] * 3 + [pl.BlockSpec((tm, D_HYENA), row)] * 2
    if apply_ln:
        out_shape = [jax.ShapeDtypeStruct((n, D_MODEL), F32)] + out_shape
        out_specs = [pl.BlockSpec((tm, D_MODEL), row)] + out_specs
    return pl.pallas_call(
        functools.partial(_inproj_kernel, apply_ln, seq_len // tm),
        grid=(n // tm,),
        in_specs=[pl.BlockSpec((tm, D_MODEL), row), pl.BlockSpec((HALO, D_MODEL), prev),
                  pl.BlockSpec((HALO, D_MODEL), nxt), pl.BlockSpec((1, D_MODEL), fixed),
                  pl.BlockSpec((1, D_MODEL), fixed), pl.BlockSpec((D_MODEL, D_QKVH), fixed),
                  pl.BlockSpec((1, D_QKVH), fixed), pl.BlockSpec((3, 3 * D_HYENA), fixed),
                  pl.BlockSpec((1, 3 * D_HYENA), fixed)],
        out_specs=out_specs,
        out_shape=out_shape,
        compiler_params=_params("parallel"),
        name="inproj",
    )(x, x, x, ln_g, ln_b, w, bias, short_w, short_b)


def _na_bias_table(rpb):
    t = np.arange(GRID_W)
    kc = np.arange(GRID_W)
    col_start = np.clip(t - NA_WIN_COLS // 2, 0, GRID_W - NA_WIN_COLS)
    valid = (kc[None, :] >= col_start[:, None]) & (kc[None, :] < col_start[:, None] + NA_WIN_COLS)
    col_off = np.clip(kc[None, :] - t[:, None], 1 - NA_WIN_COLS, NA_WIN_COLS - 1) + (NA_WIN_COLS - 1)
    n_off = 2 * NA_WIN_COLS - 1
    pick = (col_off[:, :, None] == np.arange(n_off)[None, None, :]).astype(np.float32)
    g = jnp.einsum('hjs,tks->htjk', rpb.astype(F32), pick, precision=lax.Precision.HIGHEST)
    g = jnp.where(jnp.asarray(valid)[None, :, None, :], g * LOG2E, NEG_BIG)
    top = NA_WIN_ROWS - 1
    g = jnp.stack([g[:, :, top - c:top - c + NA_WIN_ROWS, :] for c in range(NA_WIN_ROWS)], axis=0)
    return g.reshape(NA_WIN_ROWS, NA_HEADS * GRID_W, NA_WIN_ROWS * GRID_W)


def _head_mask():
    r = np.arange(NA_HEADS * GRID_W)[:, None] // GRID_W
    c = np.arange(D_ATTN)[None, :] // NA_HEAD_DIM
    return (r == c).astype(np.float32)


NA_ROWS_PER_STEP = 4


def _na_row(rows, r, q, k_ref, v_ref, bias, hm):
    first = jnp.clip(r - NA_WIN_ROWS // 2, 0, rows - NA_WIN_ROWS)
    start = pl.multiple_of(first * GRID_W, GRID_W)
    nkeys = NA_WIN_ROWS * GRID_W
    kw = k_ref[pl.ds(start, nkeys), :]
    vw = v_ref[pl.ds(start, nkeys), :]
    pair_mask = hm[0:2 * GRID_W, 0:LANES]
    parts = []
    for j in range(D_ATTN // LANES):
        qj = q[:, j * LANES:(j + 1) * LANES]
        qbd = jnp.concatenate([qj, qj], axis=0) * pair_mask
        parts.append(lax.dot_general(qbd, kw[:, j * LANES:(j + 1) * LANES], (((1,), (1,)), ((), ())),
                                     preferred_element_type=F32))
    s = jnp.concatenate(parts, axis=0) + bias
    e = jnp.exp2(s - jnp.max(s, axis=-1, keepdims=True))
    inv = 1.0 / jnp.sum(e, axis=-1, keepdims=True)
    p = e.astype(BF16)
    low_half = lax.broadcasted_iota(jnp.int32, (GRID_W, LANES), 1) < NA_HEAD_DIM
    tiles = []
    for j in range(D_ATTN // LANES):
        pair = slice(2 * j * GRID_W, (2 * j + 2) * GRID_W)
        o = jnp.dot(p[pair], vw[:, j * LANES:(j + 1) * LANES], preferred_element_type=F32) * inv[pair]
        tiles.append(jnp.where(low_half, o[0:GRID_W], o[GRID_W:]))
    return jnp.concatenate(tiles, axis=-1)


def _na_kernel(rows, q_ref, k_ref, v_ref, *rest):
    bias_refs, (hm_ref, o_ref) = rest[:NA_ROWS_PER_STEP], rest[NA_ROWS_PER_STEP:]
    hm = hm_ref[...]
    for j in range(NA_ROWS_PER_STEP):
        r = pl.program_id(1) * NA_ROWS_PER_STEP + j
        tok = slice(j * GRID_W, (j + 1) * GRID_W)
        o_ref[tok, :] = _na_row(rows, r, q_ref[tok, :], k_ref, v_ref, bias_refs[j][...], hm).astype(BF16)


def _neighbourhood_attention(q, k, v, bias_table, head_mask):
    b, l, _ = q.shape
    rows = l // GRID_W
    assert rows >= NA_WIN_ROWS and l % GRID_W == 0 and rows % NA_ROWS_PER_STEP == 0
    assert 2 * NA_HEAD_DIM == LANES
    nkeys = NA_WIN_ROWS * GRID_W
    nq = NA_HEADS * GRID_W
    tq = NA_ROWS_PER_STEP * GRID_W

    def cls(j):
        def index(bi, i):
            r = i * NA_ROWS_PER_STEP + j
            return (r - jnp.clip(r - NA_WIN_ROWS // 2, 0, rows - NA_WIN_ROWS), 0, 0)
        return index

    return pl.pallas_call(
        functools.partial(_na_kernel, rows),
        grid=(b, rows // NA_ROWS_PER_STEP),
        in_specs=[pl.BlockSpec((None, tq, D_ATTN), lambda bi, i: (bi, i, 0)),
                  pl.BlockSpec((None, l, D_ATTN), lambda bi, i: (bi, 0, 0)),
                  pl.BlockSpec((None, l, D_ATTN), lambda bi, i: (bi, 0, 0))]
                 + [pl.BlockSpec((None, nq, nkeys), cls(j)) for j in range(NA_ROWS_PER_STEP)]
                 + [pl.BlockSpec((nq, D_ATTN), lambda bi, i: (0, 0))],
        out_specs=pl.BlockSpec((None, tq, D_ATTN), lambda bi, i: (bi, i, 0)),
        out_shape=jax.ShapeDtypeStruct((b, l, D_ATTN), BF16),
        compiler_params=_params("parallel", "arbitrary"),
        name="nattn",
    )(q, k, v, *([bias_table] * NA_ROWS_PER_STEP), head_mask)


def _filter_features(l):
    t = jnp.linspace(0.0, 1.0, l, dtype=F32)[:, None]
    bands = (FILTER_EMB - 1) // 2
    f = jnp.linspace(1e-4, bands - 1, bands, dtype=F32)[None, :]
    ang = f * (2.0 * math.pi / l) * jnp.arange(l, dtype=F32)[:, None]
    z = jnp.concatenate([t, jnp.cos(ang), -jnp.sin(ang)], axis=-1)
    z = jnp.concatenate([z, z[0:1], z[1:][::-1]], axis=0)
    return jnp.pad(z, ((0, 0), (0, LANES - FILTER_EMB)))


def _decay_rates():
    max_decay = math.log(DECAY_TARGET) / DECAY_FAST
    min_decay = math.log(DECAY_TARGET) / DECAY_SLOW
    return jnp.abs(jnp.linspace(min_decay, max_decay, D_HYENA, dtype=F32))[None, :]


def _filter_kernel(l, z_ref, w1_ref, b1_ref, w2_ref, b2_ref, w3_ref, b3_ref, w4_ref, freq_ref,
                   rate_ref, o_ref):
    hp = lax.Precision.HIGHEST
    z = z_ref[...]
    freq = freq_ref[...]
    h = jnp.sin(freq * (jnp.dot(z, w1_ref[...], precision=hp, preferred_element_type=F32) + b1_ref[...]))
    h = jnp.sin(freq * (jnp.dot(h, w2_ref[...], precision=hp, preferred_element_type=F32) + b2_ref[...]))
    h = jnp.sin(freq * (jnp.dot(h, w3_ref[...], precision=hp, preferred_element_type=F32) + b3_ref[...]))
    w4 = w4_ref[...]
    h_hi, w_hi = h.astype(BF16), w4.astype(BF16)
    h_lo, w_lo = (h - h_hi.astype(F32)).astype(BF16), (w4 - w_hi.astype(F32)).astype(BF16)
    h = (jnp.dot(h_hi, w_hi, preferred_element_type=F32) + jnp.dot(h_lo, w_hi, preferred_element_type=F32)
         + jnp.dot(h_hi, w_lo, preferred_element_type=F32))
    hr = z.shape[0]
    for p in range(2):
        r = (pl.program_id(0) * 2 + p) * hr + lax.broadcasted_iota(jnp.int32, (hr, D_HYENA), 0)
        hp_ = h[:, p * 2 * D_HYENA:(p + 1) * 2 * D_HYENA]
        decay = jnp.exp(-z[:, p * LANES:p * LANES + 1] * rate_ref[...])
        tap = jnp.where(r < l, hp_[:, 0:D_HYENA], jnp.where(r > l, hp_[:, D_HYENA:], 0.0))
        o_ref[p * hr:(p + 1) * hr, :] = tap * decay * (0.5 / l)


def _filter_taps(l, z, w1, b1, w2, b2, w3, b3, w4, freq, rates):
    tr = _row_tile(2 * l, 512)
    hr = tr // 2
    nt = 2 * l // tr
    fixed = lambda i: (0, 0)
    full = lambda a: pl.BlockSpec(a.shape, fixed)
    side_by_side = lambda w: jnp.kron(jnp.eye(2, dtype=F32), w)
    twice = lambda v: jnp.tile(v, (1, 2))
    zp = z.reshape(nt, 2, hr, LANES).transpose(0, 2, 1, 3).reshape(nt * hr, 2 * LANES)
    args = (side_by_side(w1), twice(b1), side_by_side(w2), twice(b2), side_by_side(w3), twice(b3),
            side_by_side(w4), twice(freq), rates)
    return pl.pallas_call(
        functools.partial(_filter_kernel, l),
        grid=(nt,),
        in_specs=[pl.BlockSpec((hr, 2 * LANES), lambda i: (i, 0))] + [full(a) for a in args],
        out_specs=pl.BlockSpec((tr, D_HYENA), lambda i: (i, 0)),
        out_shape=jax.ShapeDtypeStruct((2 * l, D_HYENA), F32),
        compiler_params=_params("parallel"),
        name="filter_taps",
    )(zp, *args)


class _DftPlan:
    def __init__(self, l):
        n = 2 * l
        lg = n.bit_length() - 1
        assert 1 << lg == n
        self.n1 = n1 = 1 << (lg // 2)
        self.n2 = n2 = n // n1
        self.half = half = n1 // 2
        assert half % 16 == 0
        k1 = np.arange(n1)
        f1 = np.exp(-2j * np.pi * np.outer(k1, np.arange(n1)) / n1)
        f1h = f1[:, :half]
        self.a1_data = np.block([[f1h.real, -f1h.imag], [f1h.imag, f1h.real]]).astype(np.float32)
        self.a1_filt = np.concatenate([f1.real, f1.imag], axis=0).astype(np.float32)
        e = np.exp(2j * np.pi * np.outer(np.arange(half), k1) / n1)
        self.a3 = np.block([[e.real, -e.imag], [e.imag, e.real]]).astype(np.float32)
        tw = np.exp(-2j * np.pi * np.outer(k1, np.arange(n2)) / n)
        self.tw_r = tw.real.astype(np.float32)
        self.tw_i = tw.imag.astype(np.float32)
        f2 = np.exp(-2j * np.pi * np.outer(np.arange(n2), np.arange(n2)) / n2)
        self.f2_r = f2.real.astype(np.float32)
        self.f2_i = f2.imag.astype(np.float32)


N2_GROUP = 8
LANE_CHUNKS = D_HYENA // LANES
GROUP_COLS = N2_GROUP * D_HYENA


def _group_rows(ref):
    return ref.reshape(ref.shape[0] * ref.shape[1] * N2_GROUP, LANES)


def _load_group_row(ref, j):
    return _group_rows(ref)[pl.ds(j, ref.shape[0] * ref.shape[1], stride=N2_GROUP), :]


def _store_group_row(ref, j, val):
    _group_rows(ref)[pl.ds(j, ref.shape[0] * ref.shape[1], stride=N2_GROUP), :] = val


def _gather_group(refs):
    return jnp.concatenate([_load_group_row(refs[q], j)
                            for j in range(N2_GROUP) for q in range(LANE_CHUNKS)], axis=-1)


def _dft_rows_kernel(a_ref, *refs):
    x_refs, o_refs = refs[:LANE_CHUNKS], refs[LANE_CHUNKS:]
    res = jnp.dot(a_ref[...], _gather_group(x_refs).astype(BF16), preferred_element_type=F32)
    for q, o_ref in enumerate(o_refs):
        for j in range(N2_GROUP):
            lo = j * D_HYENA + q * LANES
            _store_group_row(o_ref, j, res[:, lo:lo + LANES])


def _lane_chunk_specs(rows):
    def spec(q):
        return pl.BlockSpec((None, 2, rows, None, N2_GROUP, LANES), lambda pi, g: (pi, 0, 0, g, 0, q))
    return [spec(q) for q in range(LANE_CHUNKS)]


def _dft_rows(a, x, n2):
    p, _, half, _, _ = x.shape
    n1 = a.shape[0] // 2
    groups = n2 // N2_GROUP
    group_blk = pl.BlockSpec((None, 2, n1, None, N2_GROUP, LANES), lambda pi, g: (pi, 0, 0, g, 0, 0))
    x6 = x.reshape(p, 2, half, groups, N2_GROUP, D_HYENA)
    outs = pl.pallas_call(
        _dft_rows_kernel,
        grid=(p, groups),
        in_specs=[pl.BlockSpec(a.shape, lambda pi, g: (0, 0))] + _lane_chunk_specs(half),
        out_specs=[group_blk] * LANE_CHUNKS,
        out_shape=[jax.ShapeDtypeStruct((p, 2, n1, groups, N2_GROUP, LANES), F32)] * LANE_CHUNKS,
        compiler_params=_params("parallel", "parallel"),
        name="dft_rows",
    )(a, *([x6] * LANE_CHUNKS))
    return [o.reshape(p, 2, n1, n2, LANES) for o in outs]


def _twiddled_dft(f2r, f2i, twr, twi):
    gr = f2r * twr - f2i * twi
    gi = f2r * twi + f2i * twr
    return gr, gi


def _chunks(refs, lead):
    return jnp.concatenate([r[lead] for r in refs], axis=-1)


def _filter_spectrum_kernel(kb, *refs):
    z_refs = refs[:LANE_CHUNKS]
    f2r_ref, f2i_ref, twr_ref, twi_ref, o_ref = refs[LANE_CHUNKS:]
    f2r = f2r_ref[...]
    f2i = f2i_ref[...]

    def body(k, carry):
        gr, gi = _twiddled_dft(f2r, f2i, twr_ref[k], twi_ref[k])
        a_re = jnp.concatenate([gr, gi], axis=0).astype(BF16)
        a_im = jnp.concatenate([-gi, gr], axis=0).astype(BF16)
        y = (jnp.dot(a_re, _chunks(z_refs, (0, k)).astype(BF16), preferred_element_type=F32)
             + jnp.dot(a_im, _chunks(z_refs, (1, k)).astype(BF16), preferred_element_type=F32))
        n2 = f2r.shape[0]
        o_ref[0, k] = y[0:n2]
        o_ref[1, k] = y[n2:]
        return carry

    lax.fori_loop(0, kb, body, 0, unroll=True)


def _filter_spectrum(plan, z_chunks):
    _, _, n1, n2, _ = z_chunks[0].shape
    kb = min(n1, 8)
    fixed = lambda i: (0, 0)
    return pl.pallas_call(
        functools.partial(_filter_spectrum_kernel, kb),
        grid=(n1 // kb,),
        in_specs=[pl.BlockSpec((None, 2, kb, n2, LANES), lambda i: (0, 0, i, 0, 0))] * LANE_CHUNKS
                 + [pl.BlockSpec((n2, n2), fixed), pl.BlockSpec((n2, n2), fixed),
                    pl.BlockSpec((kb, 1, n2), lambda i: (i, 0, 0)), pl.BlockSpec((kb, 1, n2), lambda i: (i, 0, 0))],
        out_specs=pl.BlockSpec((2, kb, n2, D_HYENA), lambda i: (0, i, 0, 0)),
        out_shape=jax.ShapeDtypeStruct((2, n1, n2, D_HYENA), F32),
        compiler_params=_params("parallel"),
        name="filter_spectrum",
    )(*z_chunks, plan.f2_r, plan.f2_i, plan.tw_r[:, None, :], plan.tw_i[:, None, :])


def _spectral_kernel(kb, *refs):
    z_refs = refs[:LANE_CHUNKS]
    kf_ref, f2r_ref, f2i_ref, twr_ref, twi_ref, twrc_ref, twic_ref = refs[LANE_CHUNKS:LANE_CHUNKS + 7]
    o_refs = refs[LANE_CHUNKS + 7:]
    f2r = f2r_ref[...]
    f2i = f2i_ref[...]
    n2 = f2r.shape[0]

    def body(k, carry):
        gr, gi = _twiddled_dft(f2r, f2i, twr_ref[k], twi_ref[k])
        a_re = jnp.concatenate([gr, gi], axis=0).astype(BF16)
        a_im = jnp.concatenate([-gi, gr], axis=0).astype(BF16)
        y = (jnp.dot(a_re, _chunks(z_refs, (0, k)).astype(BF16), preferred_element_type=F32)
             + jnp.dot(a_im, _chunks(z_refs, (1, k)).astype(BF16), preferred_element_type=F32))
        yr, yi = y[0:n2], y[n2:]
        kr, ki = kf_ref[0, k], kf_ref[1, k]
        wr = (yr * kr - yi * ki).astype(BF16)
        wi = (yr * ki + yi * kr).astype(BF16)
        cr, ci = _twiddled_dft(f2r, f2i, twrc_ref[k], twic_ref[k])
        b_re = jnp.concatenate([cr, -ci], axis=0).astype(BF16)
        b_im = jnp.concatenate([ci, cr], axis=0).astype(BF16)
        v = (jnp.dot(b_re, wr, preferred_element_type=F32)
             + jnp.dot(b_im, wi, preferred_element_type=F32))
        for q, o_ref in enumerate(o_refs):
            o_ref[0, k] = v[0:n2, q * LANES:(q + 1) * LANES]
            o_ref[1, k] = v[n2:, q * LANES:(q + 1) * LANES]
        return carry

    lax.fori_loop(0, kb, body, 0, unroll=True)


def _spectral_multiply(plan, z_chunks, kf):
    p, _, n1, n2, _ = z_chunks[0].shape
    kb = min(n1, 8)
    zblk = pl.BlockSpec((None, 2, kb, n2, LANES), lambda i, pi: (pi, 0, i, 0, 0))
    fixed = lambda i, pi: (0, 0)
    trow = pl.BlockSpec((kb, 1, n2), lambda i, pi: (i, 0, 0))
    tcol = pl.BlockSpec((kb, n2, 1), lambda i, pi: (i, 0, 0))
    return pl.pallas_call(
        functools.partial(_spectral_kernel, kb),
        grid=(n1 // kb, p),
        in_specs=[zblk] * LANE_CHUNKS
                 + [pl.BlockSpec((2, kb, n2, D_HYENA), lambda i, pi: (0, i, 0, 0)),
                    pl.BlockSpec((n2, n2), fixed), pl.BlockSpec((n2, n2), fixed),
                    trow, trow, tcol, tcol],
        out_specs=[zblk] * LANE_CHUNKS,
        out_shape=[jax.ShapeDtypeStruct(z_chunks[0].shape, F32)] * LANE_CHUNKS,
        compiler_params=_params("parallel", "arbitrary"),
        name="spectral_multiply",
    )(*z_chunks, kf, plan.f2_r, plan.f2_i, plan.tw_r[:, None, :], plan.tw_i[:, None, :],
      plan.tw_r[:, :, None], plan.tw_i[:, :, None])


def _idft_rows_kernel(a_ref, *refs):
    v_refs, u_refs = refs[:LANE_CHUNKS], refs[LANE_CHUNKS:2 * LANE_CHUNKS]
    bias_ref = refs[2 * LANE_CHUNKS]
    o_refs = refs[2 * LANE_CHUNKS + 1:]
    y = jnp.dot(a_ref[...], _gather_group(v_refs).astype(BF16), preferred_element_type=F32)
    for q, o_ref in enumerate(o_refs):
        bias = bias_ref[:, q * LANES:(q + 1) * LANES]
        for j in range(N2_GROUP):
            lo = j * D_HYENA + q * LANES
            _store_group_row(o_ref, j, y[:, lo:lo + LANES] + _load_group_row(u_refs[q], j) * bias)


def _idft_rows(a3, v_chunks, u, bias):
    p, _, half, n2, _ = u.shape
    _, _, n1, _, _ = v_chunks[0].shape
    groups = n2 // N2_GROUP
    u6 = u.reshape(p, 2, half, groups, N2_GROUP, D_HYENA)
    vblk = pl.BlockSpec((None, 2, n1, None, N2_GROUP, LANES), lambda pi, g: (pi, 0, 0, g, 0, 0))
    yblk = pl.BlockSpec((None, 2, half, None, N2_GROUP, LANES), lambda pi, g: (pi, 0, 0, g, 0, 0))
    outs = pl.pallas_call(
        _idft_rows_kernel,
        grid=(p, groups),
        in_specs=[pl.BlockSpec(a3.shape, lambda pi, g: (0, 0))] + [vblk] * LANE_CHUNKS
                 + _lane_chunk_specs(half) + [pl.BlockSpec((1, D_HYENA), lambda pi, g: (0, 0))],
        out_specs=[yblk] * LANE_CHUNKS,
        out_shape=[jax.ShapeDtypeStruct((p, 2, half, groups, N2_GROUP, LANES), F32)] * LANE_CHUNKS,
        compiler_params=_params("parallel", "parallel"),
        name="idft_rows",
    )(a3, *[v.reshape(p, 2, n1, groups, N2_GROUP, LANES) for v in v_chunks], *([u6] * LANE_CHUNKS), bias)
    return [o.reshape(p * 2 * half * n2, LANES) for o in outs]


def _hyena_long_conv(u, plan, kf, flt_bias):
    b, l, _ = u.shape
    assert b % 2 == 0 and plan.n2 % N2_GROUP == 0
    pairs = u.reshape(b // 2, 2, plan.half, plan.n2, D_HYENA)
    z = _dft_rows(jnp.asarray(plan.a1_data, BF16), pairs, plan.n2)
    v = _spectral_multiply(plan, z, kf)
    return _idft_rows(jnp.asarray(plan.a3, BF16), v, pairs, flt_bias.astype(F32)[None, :])


def _hyena_filter_spectrum(l, plan, z_feat, w1, b1, w2, b2, w3, b3, w4, freq):
    w1p = jnp.pad(w1, ((0, LANES - FILTER_EMB), (0, 0)))
    taps = _filter_taps(l, z_feat, w1p, b1[None, :], w2, b2[None, :], w3, b3[None, :], w4,
                        freq[None, :], _decay_rates())
    z = _dft_rows(jnp.asarray(plan.a1_filt, BF16), taps.reshape(1, 2, plan.half, plan.n2, D_HYENA), plan.n2)
    return _filter_spectrum(plan, z)


def _sigmoid(x):
    return 1.0 / (1.0 + jnp.exp(-x))


def _merge_kernel(h_ref, oa_ref, *refs):
    yh_refs = refs[:LANE_CHUNKS]
    x0_ref, wg_ref, bg_ref, wa_ref, wh_ref, wo_ref, g_ref, b_ref, o_ref = refs[LANE_CHUNKS:]
    h = h_ref[...]
    hb = h.astype(BF16)
    ga = jnp.dot(hb, wg_ref[:, 0:D_MODEL], preferred_element_type=F32) + bg_ref[:, 0:D_MODEL]
    gh = jnp.dot(hb, wg_ref[:, D_MODEL:], preferred_element_type=F32) + bg_ref[:, D_MODEL:]
    y_hyena = jnp.concatenate([r[...] for r in yh_refs], axis=-1)
    o_hyena = (y_hyena * x0_ref[...].astype(F32)).astype(BF16)
    merged = (_sigmoid(ga) * jnp.dot(oa_ref[...], wa_ref[...], preferred_element_type=F32)
              + _sigmoid(gh) * jnp.dot(o_hyena, wh_ref[...], preferred_element_type=F32))
    mix = jnp.dot(merged.astype(BF16), wo_ref[...], preferred_element_type=F32)
    o_ref[...] = _layer_norm(ALPHA * h + mix, g_ref[...], b_ref[...])


def _merge(h, o_attn, y_hyena, x0, wg, bg, wa, wh, wo, ln_g, ln_b):
    n = h.shape[0]
    tm = _row_tile(n, 512)
    row = lambda i: (i, 0)
    fixed = lambda i: (0, 0)
    full = lambda a: pl.BlockSpec(a.shape, fixed)
    return pl.pallas_call(
        _merge_kernel,
        grid=(n // tm,),
        in_specs=[pl.BlockSpec((tm, D_MODEL), row), pl.BlockSpec((tm, D_ATTN), row)]
                 + [pl.BlockSpec((tm, LANES), row)] * LANE_CHUNKS + [pl.BlockSpec((tm, D_HYENA), row)]
                 + [full(a) for a in (wg, bg, wa, wh, wo, ln_g, ln_b)],
        out_specs=pl.BlockSpec((tm, D_MODEL), row),
        out_shape=jax.ShapeDtypeStruct((n, D_MODEL), F32),
        compiler_params=_params("parallel"),
        name="merge",
    )(h, o_attn, *y_hyena, x0, wg, bg, wa, wh, wo, ln_g, ln_b)


ROUTE_TILE = 512
ROW_TILE = 512
EXPERT_ROWS = 512
SLAB = 8
ISSUE_UNROLL = 8
assert SLAB * LANES == D_MODEL
L_E1, L_E2, L_G1, L_G2, L_R1, L_R2 = 0, 1, 2, 3, 4, 5


def _router_kernel(x_ref, w_ref, b_ref, route_ref, route_t_ref, count_ref):
    x, w = x_ref[...], w_ref[...]
    x_hi, w_hi = x.astype(BF16), w.astype(BF16)
    x_lo, w_lo = (x - x_hi.astype(F32)).astype(BF16), (w - w_hi.astype(F32)).astype(BF16)
    logits = (jnp.dot(x_hi, w_hi, preferred_element_type=F32) + jnp.dot(x_lo, w_hi, preferred_element_type=F32)
              + jnp.dot(x_hi, w_lo, preferred_element_type=F32)) + b_ref[...]
    tm = logits.shape[0]
    lane_i = lax.broadcasted_iota(jnp.int32, logits.shape, 1)
    lane = lane_i.astype(F32)
    group_of_lane = jnp.right_shift(lane_i, 2).astype(F32)
    neg = -jnp.inf

    def first_argmax(vals, vmax):
        return jnp.min(jnp.where(vals == vmax, lane, float(LANES)), axis=-1, keepdims=True)

    gl = jnp.where((lane_i >= N_EXPERTS) & (lane_i < N_EXPERTS + N_GROUPS), logits, neg)
    gmax = jnp.max(gl, axis=-1, keepdims=True)
    p_group = 1.0 / jnp.sum(jnp.exp(gl - gmax), axis=-1, keepdims=True)
    g_sel = first_argmax(gl, gmax) - float(N_EXPERTS)
    el = jnp.where((lane_i < N_EXPERTS) & (group_of_lane == g_sel), logits, neg)
    m1 = jnp.max(el, axis=-1, keepdims=True)
    i1 = first_argmax(el, m1)
    el2 = jnp.where(lane == i1, neg, el)
    m2 = jnp.max(el2, axis=-1, keepdims=True)
    i2 = first_argmax(el2, m2)
    e2 = jnp.exp(m2 - m1)
    denom = 1.0 / (1.0 + e2)
    gate1 = denom * p_group
    gate2 = e2 * denom * p_group
    hit1 = lane == i1
    hit2 = lane == i2
    onehot = jnp.where(hit1 | hit2, 1.0, 0.0)
    earlier = (lax.broadcasted_iota(jnp.int32, (tm, tm), 0) > lax.broadcasted_iota(jnp.int32, (tm, tm), 1))
    before = jnp.dot(jnp.where(earlier, 1.0, 0.0).astype(BF16), onehot.astype(BF16), preferred_element_type=F32)
    rank1 = jnp.sum(jnp.where(hit1, before, 0.0), axis=-1, keepdims=True)
    rank2 = jnp.sum(jnp.where(hit2, before, 0.0), axis=-1, keepdims=True)
    out = jnp.zeros_like(logits)
    for l, val in ((L_E1, i1), (L_E2, i2), (L_G1, gate1), (L_G2, gate2), (L_R1, rank1), (L_R2, rank2)):
        out = jnp.where(lane_i == l, val, out)
    route_ref[...] = out
    route_t_ref[...] = out.T[0:8, :]
    count_ref[...] = jnp.broadcast_to(jnp.sum(onehot, axis=0, keepdims=True), count_ref.shape)


def _router(x, w, b):
    n = x.shape[0]
    tm = _row_tile(n, ROUTE_TILE)
    nt = n // tm
    return pl.pallas_call(
        _router_kernel,
        grid=(nt,),
        in_specs=[pl.BlockSpec((tm, D_MODEL), lambda i: (i, 0)),
                  pl.BlockSpec((D_MODEL, LANES), lambda i: (0, 0)),
                  pl.BlockSpec((1, LANES), lambda i: (0, 0))],
        out_specs=[pl.BlockSpec((tm, LANES), lambda i: (i, 0)), pl.BlockSpec((8, tm), lambda i: (0, i)),
                   pl.BlockSpec((None, 8, LANES), lambda i: (i, 0, 0))],
        out_shape=[jax.ShapeDtypeStruct((n, LANES), F32), jax.ShapeDtypeStruct((8, n), F32),
                   jax.ShapeDtypeStruct((nt, 8, LANES), F32)],
        compiler_params=_params("parallel"),
        name="router",
    )(x, w, b)


def _routing_tables(route_t, counts, n):
    tm = _row_tile(n, ROUTE_TILE)
    nt = n // tm
    cnt = counts[:, 0, :N_EXPERTS].astype(jnp.int32)
    total = jnp.sum(cnt, axis=0)
    padded = (total + EXPERT_ROWS - 1) // EXPERT_ROWS * EXPERT_ROWS
    ends = jnp.cumsum(padded)
    tile_base = (ends - padded)[None, :] + jnp.cumsum(cnt, axis=0) - cnt
    base_tok = jnp.broadcast_to(tile_base.T[:, :, None], (N_EXPERTS, nt, tm)).reshape(N_EXPERTS, n)
    experts = jnp.arange(N_EXPERTS, dtype=jnp.int32)[:, None]
    route_i = route_t.astype(jnp.int32)

    def position(l_e, l_r):
        return jnp.sum(jnp.where(route_i[l_e][None, :] == experts, base_tok, 0), axis=0) + route_i[l_r]

    pos = jnp.stack([position(L_E1, L_R1), position(L_E2, L_R2)], axis=1).reshape(2 * n)
    n_blocks = 2 * n // EXPERT_ROWS + N_EXPERTS
    block_start = jnp.arange(n_blocks, dtype=jnp.int32) * EXPERT_ROWS
    block_expert = jnp.sum((ends[None, :] <= block_start[:, None]).astype(jnp.int32), axis=1)
    block_valid = (block_expert < N_EXPERTS).astype(jnp.int32)
    last_block = jnp.where(padded > total, ends // EXPERT_ROWS - 1, -1)
    tail = ends[-1] // EXPERT_ROWS + jnp.arange(N_EXPERTS, dtype=jnp.int32)
    zero_blocks = jnp.concatenate([last_block, jnp.where(tail < n_blocks, tail, -1)]).astype(jnp.int32)
    return pos, jnp.minimum(block_expert, N_EXPERTS - 1), block_valid, zero_blocks, n_blocks


def _to_slabs(slab_ref, base, x):
    rows = x.shape[0]
    for s in range(SLAB):
        slab_ref[pl.ds(base + s, rows, stride=SLAB), :] = x[:, s * LANES:(s + 1) * LANES]


def _from_slabs(slab_ref, base, rows):
    return jnp.concatenate([slab_ref[pl.ds(base + s, rows, stride=SLAB), :] for s in range(SLAB)], axis=-1)


def _dispatch_kernel(pos_ref, zb_ref, x_ref, xs_ref, slab_a, slab_b, sems):
    tm = x_ref.shape[0] // 2
    step = pl.program_id(0)

    def tile_wait(slab_ref, sem):
        for _ in range(2):
            pltpu.make_async_copy(slab_ref, xs_ref.at[pl.ds(0, tm * SLAB)], sem).wait()

    @pl.when(step == 0)
    def _():
        slab_a[...] = jnp.zeros_like(slab_a)
        block = slab_a.shape[0]
        for j in range(zb_ref.shape[0]):
            @pl.when(zb_ref[j] >= 0)
            def _():
                pltpu.make_async_copy(slab_a, xs_ref.at[pl.ds(zb_ref[j] * block, block)], sems.at[0]).start()
        for j in range(zb_ref.shape[0]):
            @pl.when(zb_ref[j] >= 0)
            def _():
                pltpu.make_async_copy(slab_a, xs_ref.at[pl.ds(0, block)], sems.at[0]).wait()

    for half, slab_ref in enumerate((slab_a, slab_b)):
        sem = sems.at[half]

        @pl.when(step > 0)
        def _():
            tile_wait(slab_ref, sem)

        _to_slabs(slab_ref, 0, x_ref[half * tm:(half + 1) * tm, :])

        def issue(t, carry):
            for k in range(2):
                p = pos_ref[2 * (half * tm + t) + k]
                pltpu.make_async_copy(slab_ref.at[pl.ds(t * SLAB, SLAB)], xs_ref.at[pl.ds(p * SLAB, SLAB)],
                                      sem).start(priority=k)
            return carry

        lax.fori_loop(0, tm, issue, 0, unroll=ISSUE_UNROLL)

    @pl.when(step == pl.num_programs(0) - 1)
    def _():
        tile_wait(slab_a, sems.at[0])
        tile_wait(slab_b, sems.at[1])


def _dispatch(x, pos, zero_blocks, n_rows):
    n = x.shape[0]
    tm = _row_tile(n, ROW_TILE)
    assert tm == EXPERT_ROWS
    assert n % (2 * tm) == 0
    return pl.pallas_call(
        _dispatch_kernel,
        grid=(n // (2 * tm),),
        in_specs=[pl.BlockSpec((4 * tm,), lambda i: (i,), memory_space=pltpu.SMEM),
                  pl.BlockSpec(zero_blocks.shape, lambda i: (0,), memory_space=pltpu.SMEM),
                  pl.BlockSpec((2 * tm, D_MODEL), lambda i: (i, 0))],
        out_specs=pl.BlockSpec(memory_space=pl.ANY),
        out_shape=jax.ShapeDtypeStruct((n_rows * SLAB, LANES), F32),
        scratch_shapes=[pltpu.VMEM((tm * SLAB, LANES), F32), pltpu.VMEM((tm * SLAB, LANES), F32),
                        pltpu.SemaphoreType.DMA((2,))],
        compiler_params=_params("arbitrary"),
        name="dispatch",
    )(pos, zero_blocks, x)


def _expert_ffn_kernel(be_ref, bv_ref, xs_ref, w1_ref, w3_ref, w2_ref, y_ref):
    rows = xs_ref.shape[0] // SLAB
    valid = bv_ref[pl.program_id(0)] > 0

    @pl.when(valid)
    def _():
        xb = _from_slabs(xs_ref, 0, rows).astype(BF16)
        a = jnp.dot(xb, w1_ref[...], preferred_element_type=F32)
        g = jnp.dot(xb, w3_ref[...], preferred_element_type=F32)
        hidden = (a * _sigmoid(a) * g).astype(BF16)
        _to_slabs(y_ref, 0, jnp.dot(hidden, w2_ref[...], preferred_element_type=F32))

    @pl.when(jnp.logical_not(valid))
    def _():
        y_ref[...] = jnp.zeros_like(y_ref)


def _expert_ffn(xs, block_expert, block_valid, n_blocks, w1, w3, w2):
    rows = EXPERT_ROWS * SLAB
    blk = lambda b, be, bv: (b, 0)
    wsel = lambda b, be, bv: (be[b], 0, 0)
    return pl.pallas_call(
        _expert_ffn_kernel,
        grid_spec=pltpu.PrefetchScalarGridSpec(
            num_scalar_prefetch=2,
            grid=(n_blocks,),
            in_specs=[pl.BlockSpec((rows, LANES), blk),
                      pl.BlockSpec((None, D_MODEL, D_EXPERT), wsel),
                      pl.BlockSpec((None, D_MODEL, D_EXPERT), wsel),
                      pl.BlockSpec((None, D_EXPERT, D_MODEL), wsel)],
            out_specs=pl.BlockSpec((rows, LANES), blk)),
        out_shape=jax.ShapeDtypeStruct(xs.shape, F32),
        compiler_params=_params("arbitrary"),
        name="expert_ffn",
    )(block_expert, block_valid, xs, w1, w3, w2)


def _combine_kernel(pos_ref, pos_next_ref, x_ref, route_ref, g_ref, b_ref, y_ref, o_ref, slab_a, slab_b, sems):
    tm = x_ref.shape[0] // 2
    step = pl.program_id(0)

    def fetch(tile_pos_ref, half, slab_ref, sem):
        def issue(t, carry):
            for k in range(2):
                p = tile_pos_ref[2 * (half * tm + t) + k]
                pltpu.make_async_copy(y_ref.at[pl.ds(p * SLAB, SLAB)],
                                      slab_ref.at[pl.ds((k * tm + t) * SLAB, SLAB)], sem).start(priority=k)
            return carry

        lax.fori_loop(0, tm, issue, 0, unroll=ISSUE_UNROLL)

    def finish(half, slab_ref, sem):
        for k in range(2):
            pltpu.make_async_copy(y_ref.at[pl.ds(0, tm * SLAB)],
                                  slab_ref.at[pl.ds(k * tm * SLAB, tm * SLAB)], sem).wait()
        rows = slice(half * tm, (half + 1) * tm)
        route = route_ref[rows, :]
        ffn = (route[:, L_G1:L_G1 + 1] * _from_slabs(slab_ref, 0, tm)
               + route[:, L_G2:L_G2 + 1] * _from_slabs(slab_ref, tm * SLAB, tm))
        o_ref[rows, :] = _layer_norm(ALPHA * x_ref[rows, :] + ffn, g_ref[...], b_ref[...])

    @pl.when(step == 0)
    def _():
        fetch(pos_ref, 0, slab_a, sems.at[0])

    fetch(pos_ref, 1, slab_b, sems.at[1])
    finish(0, slab_a, sems.at[0])

    @pl.when(step < pl.num_programs(0) - 1)
    def _():
        fetch(pos_next_ref, 0, slab_a, sems.at[0])

    finish(1, slab_b, sems.at[1])


def _combine(x, route, pos, y, ln_g, ln_b):
    n = x.shape[0]
    tm = _row_tile(n, ROW_TILE)
    assert n % (2 * tm) == 0
    steps = n // (2 * tm)
    row = lambda i: (i, 0)
    fixed = lambda i: (0, 0)
    return pl.pallas_call(
        _combine_kernel,
        grid=(steps,),
        in_specs=[pl.BlockSpec((4 * tm,), lambda i: (i,), memory_space=pltpu.SMEM),
                  pl.BlockSpec((4 * tm,), lambda i: (jnp.minimum(i + 1, steps - 1),), memory_space=pltpu.SMEM),
                  pl.BlockSpec((2 * tm, D_MODEL), row), pl.BlockSpec((2 * tm, LANES), row),
                  pl.BlockSpec((1, D_MODEL), fixed), pl.BlockSpec((1, D_MODEL), fixed),
                  pl.BlockSpec(memory_space=pl.ANY)],
        out_specs=pl.BlockSpec((2 * tm, D_MODEL), row),
        out_shape=jax.ShapeDtypeStruct((n, D_MODEL), F32),
        scratch_shapes=[pltpu.VMEM((2 * tm * SLAB, LANES), F32), pltpu.VMEM((2 * tm * SLAB, LANES), F32),
                        pltpu.SemaphoreType.DMA((2,))],
        compiler_params=_params("arbitrary"),
        name="combine",
    )(pos, pos, x, route, ln_g, ln_b, y)


def _moe(x, lay):
    n = x.shape[0]
    route, route_t, counts = _router(x, lay["w_route"], lay["b_route"])
    pos, block_expert, block_valid, zero_blocks, n_blocks = _routing_tables(route_t, counts, n)
    xs = _dispatch(x, pos, zero_blocks, n_blocks * EXPERT_ROWS)
    y = _expert_ffn(xs, block_expert, block_valid, n_blocks, lay["w1"], lay["w3"], lay["w2"])
    return _combine(x, route, pos, y, lay["ln2_g"], lay["ln2_b"])


def _prepare_layer(i, w_in, b_in, short_w, short_b, na_rpb, w_branch_attn, w_branch_hyena, w_out,
                   ln1_g, ln1_b, w_group, b_group, w_router, b_router, w1, w3, w2, ln2_g, ln2_b, flt_bias):
    row = lambda a: a[None, :].astype(F32)
    w_route = jnp.zeros((D_MODEL, LANES), F32)
    w_route = w_route.at[:, 0:N_EXPERTS].set(w_router[i]).at[:, N_EXPERTS:N_EXPERTS + N_GROUPS].set(w_group[i])
    b_route = jnp.zeros((1, LANES), F32)
    b_route = b_route.at[0, 0:N_EXPERTS].set(b_router[i]).at[0, N_EXPERTS:N_EXPERTS + N_GROUPS].set(b_group[i])
    return dict(
        w_qkvh=w_in[i][:, :D_QKVH].astype(BF16), b_qkvh=row(b_in[i][:D_QKVH]),
        w_gates=w_in[i][:, D_QKVH:].astype(BF16), b_gates=row(b_in[i][D_QKVH:]),
        short_w=short_w[i], short_b=row(short_b[i]), flt_bias=flt_bias[i],
        na_bias=_na_bias_table(na_rpb[i]),
        wa=w_branch_attn[i].astype(BF16), wh=w_branch_hyena[i].astype(BF16), wo=w_out[i].astype(BF16),
        ln1_g=row(ln1_g[i]), ln1_b=row(ln1_b[i]), ln2_g=row(ln2_g[i]), ln2_b=row(ln2_b[i]),
        w_route=w_route, b_route=b_route,
        w1=w1[i].astype(BF16), w3=w3[i].astype(BF16), w2=w2[i].astype(BF16),
    )


def _trunk(x, ln_in_g, ln_in_b, layers, filt):
    b, l, d = x.shape
    n = b * l
    plan = _DftPlan(l)
    z_feat = _filter_features(l)
    head_mask = jnp.asarray(_head_mask(), BF16)
    h = x.reshape(n, d)
    for i, lay in enumerate(layers):
        outs = _inproj(h, l, ln_in_g[None, :], ln_in_b[None, :], lay["w_qkvh"], lay["b_qkvh"],
                       lay["short_w"], lay["short_b"], apply_ln=(i == 0))
        if i == 0:
            h, q, k, v, u, x0 = outs
        else:
            q, k, v, u, x0 = outs
        seq = lambda a: a.reshape(b, l, a.shape[-1])
        o_attn = _neighbourhood_attention(seq(q), seq(k), seq(v), lay["na_bias"], head_mask).reshape(n, D_ATTN)
        kf = _hyena_filter_spectrum(l, plan, z_feat, *[f[i] for f in filt])
        y_hyena = _hyena_long_conv(seq(u), plan, kf, lay["flt_bias"])
        h = _merge(h, o_attn, y_hyena, x0, lay["w_gates"], lay["b_gates"], lay["wa"], lay["wh"], lay["wo"],
                   lay["ln1_g"], lay["ln1_b"])
        h = _moe(h, lay)
    return h.reshape(b, l, d)


def kernel(x_prompt, x_sample, ln_in_g, ln_in_b, w_in, b_in, short_w, short_b, na_rpb, flt_w1, flt_b1, flt_w2, flt_b2, flt_w3, flt_b3, flt_w4, flt_freq, flt_bias, w_branch_attn, w_branch_hyena, w_out, ln1_g, ln1_b, w_group, b_group, w_router, b_router, w1, w3, w2, ln2_g, ln2_b):
    layers = [_prepare_layer(i, w_in, b_in, short_w, short_b, na_rpb, w_branch_attn, w_branch_hyena, w_out,
                             ln1_g, ln1_b, w_group, b_group, w_router, b_router, w1, w3, w2, ln2_g, ln2_b,
                             flt_bias) for i in range(DEPTH)]
    filt = (flt_w1, flt_b1, flt_w2, flt_b2, flt_w3, flt_b3, flt_w4, flt_freq)
    y_prompt = _trunk(x_prompt, ln_in_g, ln_in_b, layers, filt)
    y_sample = _trunk(x_sample, ln_in_g, ln_in_b, layers, filt)
    return (y_prompt, y_sample)
```

```python
import functools
import math

import numpy as np
import jax
import jax.numpy as jnp
from jax import lax
from jax.experimental import pallas as pl
from jax.experimental.pallas import tpu as pltpu

F32 = jnp.float32
BF16 = jnp.bfloat16

D_MODEL = 1024
DEPTH = 2
GRID_W = 64
NA_HEADS = 8
NA_HEAD_DIM = 64
D_ATTN = NA_HEADS * NA_HEAD_DIM
NA_WIN_ROWS = 8
NA_WIN_COLS = 16
D_HYENA = D_MODEL // 2
FILTER_EMB = 33
FILTER_HIDDEN = 64
DECAY_FAST = 0.3
DECAY_SLOW = 1.5
DECAY_TARGET = 1e-2
N_GROUPS = 4
EXPERTS_PER_GROUP = 4
N_EXPERTS = N_GROUPS * EXPERTS_PER_GROUP
D_EXPERT = 512
LN_EPS = 1e-5
ALPHA = (2.0 * DEPTH) ** 0.25
D_QKVH = 3 * D_ATTN + 3 * D_HYENA
D_GATES = 2 * D_MODEL

LANES = 128
VMEM_LIMIT = 56 * 1024 * 1024
NEG_BIG = -1e30
LOG2E = math.log2(math.e)


def _params(*sem):
    return pltpu.CompilerParams(dimension_semantics=sem, vmem_limit_bytes=VMEM_LIMIT)


def _layer_norm(x, g, b):
    mu = jnp.mean(x, axis=-1, keepdims=True)
    xc = x - mu
    var = jnp.mean(xc * xc, axis=-1, keepdims=True)
    return xc * lax.rsqrt(var + LN_EPS) * g + b


def _row_tile(n, want):
    t = min(n, want)
    assert n % t == 0, (n, t)
    return t


HALO = 8


def _inproj_kernel(apply_ln, seq_tiles, x_ref, prev_ref, next_ref, g_ref, b_ref, w_ref, bias_ref,
                   sw_ref, sb_ref, *out_refs):
    x = x_ref[...]
    halo = jnp.concatenate([prev_ref[...], next_ref[...]], axis=0)
    if apply_ln:
        h_ref, q_ref, k_ref, v_ref, u_ref, x0_ref = out_refs
        x = _layer_norm(x, g_ref[...], b_ref[...])
        halo = _layer_norm(halo, g_ref[...], b_ref[...])
        h_ref[...] = x
    else:
        q_ref, k_ref, v_ref, u_ref, x0_ref = out_refs
    xb = x.astype(BF16)
    tm = xb.shape[0]

    def proj(rows, lo, hi):
        return jnp.dot(rows, w_ref[:, lo:hi], preferred_element_type=F32) + bias_ref[:, lo:hi]

    q_ref[...] = (proj(xb, 0, D_ATTN) * (NA_HEAD_DIM ** -0.5 * LOG2E)).astype(BF16)
    k_ref[...] = proj(xb, D_ATTN, 2 * D_ATTN).astype(BF16)
    v_ref[...] = proj(xb, 2 * D_ATTN, 3 * D_ATTN).astype(BF16)

    hy = proj(jnp.concatenate([xb, halo.astype(BF16)], axis=0), 3 * D_ATTN, D_QKVH)
    cur = hy[0:tm]
    i = pl.program_id(0) % seq_tiles
    prev_row = jnp.where(i > 0, hy[tm + HALO - 1:tm + HALO], 0.0)
    next_row = jnp.where(i < seq_tiles - 1, hy[tm + HALO:tm + HALO + 1], 0.0)
    row = lax.broadcasted_iota(jnp.int32, cur.shape, 0)
    up = jnp.where(row == 0, prev_row, pltpu.roll(cur, 1, axis=0))
    dn = jnp.where(row == tm - 1, next_row, pltpu.roll(cur, tm - 1, axis=0))
    c = sw_ref[0:1, :] * up + sw_ref[1:2, :] * cur + sw_ref[2:3, :] * dn + sb_ref[...]
    u_ref[...] = c[:, 2 * D_HYENA:] * c[:, D_HYENA:2 * D_HYENA]
    x0_ref[...] = c[:, 0:D_HYENA].astype(BF16)


def _inproj(x, seq_len, ln_g, ln_b, w, bias, short_w, short_b, apply_ln):
    n = x.shape[0]
    tm = _row_tile(seq_len, 512)
    sub = tm // HALO
    row = lambda i: (i, 0)
    fixed = lambda i: (0, 0)
    prev = lambda i: (jnp.maximum(i * sub - 1, 0), 0)
    nxt = lambda i: (jnp.minimum((i + 1) * sub, n // HALO - 1), 0)
    out_shape = ([jax.ShapeDtypeStruct((n, D_ATTN), BF16)] * 3
                 + [jax.ShapeDtypeStruct((n, D_HYENA), F32), jax.ShapeDtypeStruct((n, D_HYENA), BF16)])
    out_specs = [pl.BlockSpec((tm, D_ATTN), row)] * 3 + [pl.BlockSpec((tm, D_HYENA), row)] * 2
    if apply_ln:
        out_shape = [jax.ShapeDtypeStruct((n, D_MODEL), F32)] + out_shape
        out_specs = [pl.BlockSpec((tm, D_MODEL), row)] + out_specs
    return pl.pallas_call(
        functools.partial(_inproj_kernel, apply_ln, seq_len // tm),
        grid=(n // tm,),
        in_specs=[pl.BlockSpec((tm, D_MODEL), row), pl.BlockSpec((HALO, D_MODEL), prev),
                  pl.BlockSpec((HALO, D_MODEL), nxt), pl.BlockSpec((1, D_MODEL), fixed),
                  pl.BlockSpec((1, D_MODEL), fixed), pl.BlockSpec((D_MODEL, D_QKVH), fixed),
                  pl.BlockSpec((1, D_QKVH), fixed), pl.BlockSpec((3, 3 * D_HYENA), fixed),
                  pl.BlockSpec((1, 3 * D_HYENA), fixed)],
        out_specs=out_specs,
        out_shape=out_shape,
        compiler_params=_params("parallel"),
        name="inproj",
    )(x, x, x, ln_g, ln_b, w, bias, short_w, short_b)


def _na_bias_table(rpb):
    t = np.arange(GRID_W)
    kc = np.arange(GRID_W)
    col_start = np.clip(t - NA_WIN_COLS // 2, 0, GRID_W - NA_WIN_COLS)
    valid = (kc[None, :] >= col_start[:, None]) & (kc[None, :] < col_start[:, None] + NA_WIN_COLS)
    col_off = np.clip(kc[None, :] - t[:, None], 1 - NA_WIN_COLS, NA_WIN_COLS - 1) + (NA_WIN_COLS - 1)
    n_off = 2 * NA_WIN_COLS - 1
    pick = (col_off[:, :, None] == np.arange(n_off)[None, None, :]).astype(np.float32)
    g = jnp.einsum('hjs,tks->htjk', rpb.astype(F32), pick, precision=lax.Precision.HIGHEST)
    g = jnp.where(jnp.asarray(valid)[None, :, None, :], g * LOG2E, NEG_BIG)
    top = NA_WIN_ROWS - 1
    g = jnp.stack([g[:, :, top - c:top - c + NA_WIN_ROWS, :] for c in range(NA_WIN_ROWS)], axis=0)
    return g.reshape(NA_WIN_ROWS, NA_HEADS * GRID_W, NA_WIN_ROWS * GRID_W)


def _head_mask():
    r = np.arange(NA_HEADS * GRID_W)[:, None] // GRID_W
    c = np.arange(D_ATTN)[None, :] // NA_HEAD_DIM
    return (r == c).astype(np.float32)


NA_ROWS_PER_STEP = 8


def _na_row(rows, r, q, k_ref, v_ref, bias, hm):
    first = jnp.clip(r - NA_WIN_ROWS // 2, 0, rows - NA_WIN_ROWS)
    start = pl.multiple_of(first * GRID_W, GRID_W)
    nkeys = NA_WIN_ROWS * GRID_W
    kw = k_ref[pl.ds(start, nkeys), :]
    vw = v_ref[pl.ds(start, nkeys), :]
    pair_mask = hm[0:2 * GRID_W, 0:LANES]
    parts = []
    for j in range(D_ATTN // LANES):
        qj = q[:, j * LANES:(j + 1) * LANES]
        qbd = jnp.concatenate([qj, qj], axis=0) * pair_mask
        parts.append(lax.dot_general(qbd, kw[:, j * LANES:(j + 1) * LANES], (((1,), (1,)), ((), ())),
                                     preferred_element_type=F32))
    s = jnp.concatenate(parts, axis=0) + bias
    e = jnp.exp2(s - jnp.max(s, axis=-1, keepdims=True))
    inv = 1.0 / jnp.sum(e, axis=-1, keepdims=True)
    p = e.astype(BF16)
    low_half = lax.broadcasted_iota(jnp.int32, (GRID_W, LANES), 1) < NA_HEAD_DIM
    tiles = []
    for j in range(D_ATTN // LANES):
        pair = slice(2 * j * GRID_W, (2 * j + 2) * GRID_W)
        o = jnp.dot(p[pair], vw[:, j * LANES:(j + 1) * LANES], preferred_element_type=F32) * inv[pair]
        tiles.append(jnp.where(low_half, o[0:GRID_W], o[GRID_W:]))
    return jnp.concatenate(tiles, axis=-1)


def _na_kernel(rows, q_ref, k_ref, v_ref, *rest):
    bias_refs, (hm_ref, o_ref) = rest[:NA_ROWS_PER_STEP], rest[NA_ROWS_PER_STEP:]
    hm = hm_ref[...]
    for j in range(NA_ROWS_PER_STEP):
        r = pl.program_id(1) * NA_ROWS_PER_STEP + j
        tok = slice(j * GRID_W, (j + 1) * GRID_W)
        o_ref[tok, :] = _na_row(rows, r, q_ref[tok, :], k_ref, v_ref, bias_refs[j][...], hm).astype(BF16)


def _neighbourhood_attention(q, k, v, bias_table, head_mask):
    b, l, _ = q.shape
    rows = l // GRID_W
    assert rows >= NA_WIN_ROWS and l % GRID_W == 0 and rows % NA_ROWS_PER_STEP == 0
    assert 2 * NA_HEAD_DIM == LANES
    nkeys = NA_WIN_ROWS * GRID_W
    nq = NA_HEADS * GRID_W
    tq = NA_ROWS_PER_STEP * GRID_W

    def cls(j):
        def index(bi, i):
            r = i * NA_ROWS_PER_STEP + j
            return (r - jnp.clip(r - NA_WIN_ROWS // 2, 0, rows - NA_WIN_ROWS), 0, 0)
        return index

    return pl.pallas_call(
        functools.partial(_na_kernel, rows),
        grid=(b, rows // NA_ROWS_PER_STEP),
        in_specs=[pl.BlockSpec((None, tq, D_ATTN), lambda bi, i: (bi, i, 0)),
                  pl.BlockSpec((None, l, D_ATTN), lambda bi, i: (bi, 0, 0)),
                  pl.BlockSpec((None, l, D_ATTN), lambda bi, i: (bi, 0, 0))]
                 + [pl.BlockSpec((None, nq, nkeys), cls(j)) for j in range(NA_ROWS_PER_STEP)]
                 + [pl.BlockSpec((nq, D_ATTN), lambda bi, i: (0, 0))],
        out_specs=pl.BlockSpec((None, tq, D_ATTN), lambda bi, i: (bi, i, 0)),
        out_shape=jax.ShapeDtypeStruct((b, l, D_ATTN), BF16),
        compiler_params=_params("parallel", "arbitrary"),
        name="nattn",
    )(q, k, v, *([bias_table] * NA_ROWS_PER_STEP), head_mask)


def _filter_features(l):
    t = jnp.linspace(0.0, 1.0, l, dtype=F32)[:, None]
    bands = (FILTER_EMB - 1) // 2
    f = jnp.linspace(1e-4, bands - 1, bands, dtype=F32)[None, :]
    ang = f * (2.0 * math.pi / l) * jnp.arange(l, dtype=F32)[:, None]
    z = jnp.concatenate([t, jnp.cos(ang), -jnp.sin(ang)], axis=-1)
    z = jnp.concatenate([z, z[0:1], z[1:][::-1]], axis=0)
    return jnp.pad(z, ((0, 0), (0, LANES - FILTER_EMB)))


def _decay_rates():
    max_decay = math.log(DECAY_TARGET) / DECAY_FAST
    min_decay = math.log(DECAY_TARGET) / DECAY_SLOW
    return jnp.abs(jnp.linspace(min_decay, max_decay, D_HYENA, dtype=F32))[None, :]


def _filter_kernel(l, z_ref, w1_ref, b1_ref, w2_ref, b2_ref, w3_ref, b3_ref, w4_ref, freq_ref,
                   rate_ref, o_ref):
    hp = lax.Precision.HIGHEST
    z = z_ref[...]
    freq = freq_ref[...]
    h = jnp.sin(freq * (jnp.dot(z, w1_ref[...], precision=hp, preferred_element_type=F32) + b1_ref[...]))
    h = jnp.sin(freq * (jnp.dot(h, w2_ref[...], precision=hp, preferred_element_type=F32) + b2_ref[...]))
    h = jnp.sin(freq * (jnp.dot(h, w3_ref[...], precision=hp, preferred_element_type=F32) + b3_ref[...]))
    w4 = w4_ref[...]
    h_hi, w_hi = h.astype(BF16), w4.astype(BF16)
    h_lo, w_lo = (h - h_hi.astype(F32)).astype(BF16), (w4 - w_hi.astype(F32)).astype(BF16)
    h = (jnp.dot(h_hi, w_hi, preferred_element_type=F32) + jnp.dot(h_lo, w_hi, preferred_element_type=F32)
         + jnp.dot(h_hi, w_lo, preferred_element_type=F32))
    hr = z.shape[0]
    for p in range(2):
        r = (pl.program_id(0) * 2 + p) * hr + lax.broadcasted_iota(jnp.int32, (hr, D_HYENA), 0)
        hp_ = h[:, p * 2 * D_HYENA:(p + 1) * 2 * D_HYENA]
        decay = jnp.exp(-z[:, p * LANES:p * LANES + 1] * rate_ref[...])
        tap = jnp.where(r < l, hp_[:, 0:D_HYENA], jnp.where(r > l, hp_[:, D_HYENA:], 0.0))
        o_ref[p * hr:(p + 1) * hr, :] = tap * decay * (0.5 / l)


def _filter_taps(l, z, w1, b1, w2, b2, w3, b3, w4, freq, rates):
    tr = _row_tile(2 * l, 512)
    hr = tr // 2
    nt = 2 * l // tr
    fixed = lambda i: (0, 0)
    full = lambda a: pl.BlockSpec(a.shape, fixed)
    side_by_side = lambda w: jnp.kron(jnp.eye(2, dtype=F32), w)
    twice = lambda v: jnp.tile(v, (1, 2))
    zp = z.reshape(nt, 2, hr, LANES).transpose(0, 2, 1, 3).reshape(nt * hr, 2 * LANES)
    args = (side_by_side(w1), twice(b1), side_by_side(w2), twice(b2), side_by_side(w3), twice(b3),
            side_by_side(w4), twice(freq), rates)
    return pl.pallas_call(
        functools.partial(_filter_kernel, l),
        grid=(nt,),
        in_specs=[pl.BlockSpec((hr, 2 * LANES), lambda i: (i, 0))] + [full(a) for a in args],
        out_specs=pl.BlockSpec((tr, D_HYENA), lambda i: (i, 0)),
        out_shape=jax.ShapeDtypeStruct((2 * l, D_HYENA), F32),
        compiler_params=_params("parallel"),
        name="filter_taps",
    )(zp, *args)


class _DftPlan:
    def __init__(self, l):
        n = 2 * l
        lg = n.bit_length() - 1
        assert 1 << lg == n
        self.n1 = n1 = 1 << (lg // 2)
        self.n2 = n2 = n // n1
        self.half = half = n1 // 2
        assert half % 16 == 0
        k1 = np.arange(n1)
        f1 = np.exp(-2j * np.pi * np.outer(k1, np.arange(n1)) / n1)
        f1h = f1[:, :half]
        self.a1_data = np.block([[f1h.real, -f1h.imag], [f1h.imag, f1h.real]]).astype(np.float32)
        self.a1_filt = np.concatenate([f1.real, f1.imag], axis=0).astype(np.float32)
        e = np.exp(2j * np.pi * np.outer(np.arange(half), k1) / n1)
        self.a3 = np.block([[e.real, -e.imag], [e.imag, e.real]]).astype(np.float32)
        tw = np.exp(-2j * np.pi * np.outer(k1, np.arange(n2)) / n)
        self.tw_r = tw.real.astype(np.float32)
        self.tw_i = tw.imag.astype(np.float32)
        f2 = np.exp(-2j * np.pi * np.outer(np.arange(n2), np.arange(n2)) / n2)
        self.f2_r = f2.real.astype(np.float32)
        self.f2_i = f2.imag.astype(np.float32)


N2_GROUP = 8
LANE_CHUNKS = D_HYENA // LANES
GROUP_COLS = N2_GROUP * D_HYENA


def _group_rows(ref):
    return ref.reshape(ref.shape[0] * ref.shape[1] * N2_GROUP, LANES)


def _load_group_row(ref, j):
    return _group_rows(ref)[pl.ds(j, ref.shape[0] * ref.shape[1], stride=N2_GROUP), :]


def _store_group_row(ref, j, val):
    _group_rows(ref)[pl.ds(j, ref.shape[0] * ref.shape[1], stride=N2_GROUP), :] = val


def _gather_group(refs):
    return jnp.concatenate([_load_group_row(refs[q], j)
                            for j in range(N2_GROUP) for q in range(LANE_CHUNKS)], axis=-1)


def _dft_rows_kernel(a_ref, *refs):
    x_refs, o_refs = refs[:LANE_CHUNKS], refs[LANE_CHUNKS:]
    res = jnp.dot(a_ref[...], _gather_group(x_refs).astype(BF16), preferred_element_type=F32)
    for q, o_ref in enumerate(o_refs):
        for j in range(N2_GROUP):
            lo = j * D_HYENA + q * LANES
            _store_group_row(o_ref, j, res[:, lo:lo + LANES])


def _lane_chunk_specs(rows):
    def spec(q):
        return pl.BlockSpec((None, 2, rows, None, N2_GROUP, LANES), lambda pi, g: (pi, 0, 0, g, 0, q))
    return [spec(q) for q in range(LANE_CHUNKS)]


def _dft_rows(a, x, n2):
    p, _, half, _, _ = x.shape
    n1 = a.shape[0] // 2
    groups = n2 // N2_GROUP
    group_blk = pl.BlockSpec((None, 2, n1, None, N2_GROUP, LANES), lambda pi, g: (pi, 0, 0, g, 0, 0))
    x6 = x.reshape(p, 2, half, groups, N2_GROUP, D_HYENA)
    outs = pl.pallas_call(
        _dft_rows_kernel,
        grid=(p, groups),
        in_specs=[pl.BlockSpec(a.shape, lambda pi, g: (0, 0))] + _lane_chunk_specs(half),
        out_specs=[group_blk] * LANE_CHUNKS,
        out_shape=[jax.ShapeDtypeStruct((p, 2, n1, groups, N2_GROUP, LANES), F32)] * LANE_CHUNKS,
        compiler_params=_params("parallel", "parallel"),
        name="dft_rows",
    )(a, *([x6] * LANE_CHUNKS))
    return [o.reshape(p, 2, n1, n2, LANES) for o in outs]


def _twiddled_dft(f2r, f2i, twr, twi):
    gr = f2r * twr - f2i * twi
    gi = f2r * twi + f2i * twr
    return gr, gi


def _chunks(refs, lead):
    return jnp.concatenate([r[lead] for r in refs], axis=-1)


def _filter_spectrum_kernel(kb, *refs):
    z_refs = refs[:LANE_CHUNKS]
    f2r_ref, f2i_ref, twr_ref, twi_ref, o_ref = refs[LANE_CHUNKS:]
    f2r = f2r_ref[...]
    f2i = f2i_ref[...]

    def body(k, carry):
        gr, gi = _twiddled_dft(f2r, f2i, twr_ref[k], twi_ref[k])
        a_re = jnp.concatenate([gr, gi], axis=0).astype(BF16)
        a_im = jnp.concatenate([-gi, gr], axis=0).astype(BF16)
        y = (jnp.dot(a_re, _chunks(z_refs, (0, k)).astype(BF16), preferred_element_type=F32)
             + jnp.dot(a_im, _chunks(z_refs, (1, k)).astype(BF16), preferred_element_type=F32))
        n2 = f2r.shape[0]
        o_ref[0, k] = y[0:n2]
        o_ref[1, k] = y[n2:]
        return carry

    lax.fori_loop(0, kb, body, 0, unroll=True)


def _filter_spectrum(plan, z_chunks):
    _, _, n1, n2, _ = z_chunks[0].shape
    kb = min(n1, 8)
    fixed = lambda i: (0, 0)
    return pl.pallas_call(
        functools.partial(_filter_spectrum_kernel, kb),
        grid=(n1 // kb,),
        in_specs=[pl.BlockSpec((None, 2, kb, n2, LANES), lambda i: (0, 0, i, 0, 0))] * LANE_CHUNKS
                 + [pl.BlockSpec((n2, n2), fixed), pl.BlockSpec((n2, n2), fixed),
                    pl.BlockSpec((kb, 1, n2), lambda i: (i, 0, 0)), pl.BlockSpec((kb, 1, n2), lambda i: (i, 0, 0))],
        out_specs=pl.BlockSpec((2, kb, n2, D_HYENA), lambda i: (0, i, 0, 0)),
        out_shape=jax.ShapeDtypeStruct((2, n1, n2, D_HYENA), F32),
        compiler_params=_params("parallel"),
        name="filter_spectrum",
    )(*z_chunks, plan.f2_r, plan.f2_i, plan.tw_r[:, None, :], plan.tw_i[:, None, :])


def _spectral_kernel(kb, *refs):
    z_refs = refs[:LANE_CHUNKS]
    kf_ref, f2r_ref, f2i_ref, twr_ref, twi_ref, twrc_ref, twic_ref = refs[LANE_CHUNKS:LANE_CHUNKS + 7]
    o_refs = refs[LANE_CHUNKS + 7:]
    f2r = f2r_ref[...]
    f2i = f2i_ref[...]
    n2 = f2r.shape[0]

    def body(k, carry):
        gr, gi = _twiddled_dft(f2r, f2i, twr_ref[k], twi_ref[k])
        a_re = jnp.concatenate([gr, gi], axis=0).astype(BF16)
        a_im = jnp.concatenate([-gi, gr], axis=0).astype(BF16)
        y = (jnp.dot(a_re, _chunks(z_refs, (0, k)).astype(BF16), preferred_element_type=F32)
             + jnp.dot(a_im, _chunks(z_refs, (1, k)).astype(BF16), preferred_element_type=F32))
        yr, yi = y[0:n2], y[n2:]
        kr, ki = kf_ref[0, k], kf_ref[1, k]
        wr = (yr * kr - yi * ki).astype(BF16)
        wi = (yr * ki + yi * kr).astype(BF16)
        cr, ci = _twiddled_dft(f2r, f2i, twrc_ref[k], twic_ref[k])
        b_re = jnp.concatenate([cr, -ci], axis=0).astype(BF16)
        b_im = jnp.concatenate([ci, cr], axis=0).astype(BF16)
        v = (jnp.dot(b_re, wr, preferred_element_type=F32)
             + jnp.dot(b_im, wi, preferred_element_type=F32))
        for q, o_ref in enumerate(o_refs):
            o_ref[0, k] = v[0:n2, q * LANES:(q + 1) * LANES]
            o_ref[1, k] = v[n2:, q * LANES:(q + 1) * LANES]
        return carry

    lax.fori_loop(0, kb, body, 0, unroll=True)


def _spectral_multiply(plan, z_chunks, kf):
    p, _, n1, n2, _ = z_chunks[0].shape
    kb = min(n1, 8)
    zblk = pl.BlockSpec((None, 2, kb, n2, LANES), lambda i, pi: (pi, 0, i, 0, 0))
    fixed = lambda i, pi: (0, 0)
    trow = pl.BlockSpec((kb, 1, n2), lambda i, pi: (i, 0, 0))
    tcol = pl.BlockSpec((kb, n2, 1), lambda i, pi: (i, 0, 0))
    return pl.pallas_call(
        functools.partial(_spectral_kernel, kb),
        grid=(n1 // kb, p),
        in_specs=[zblk] * LANE_CHUNKS
                 + [pl.BlockSpec((2, kb, n2, D_HYENA), lambda i, pi: (0, i, 0, 0)),
                    pl.BlockSpec((n2, n2), fixed), pl.BlockSpec((n2, n2), fixed),
                    trow, trow, tcol, tcol],
        out_specs=[zblk] * LANE_CHUNKS,
        out_shape=[jax.ShapeDtypeStruct(z_chunks[0].shape, F32)] * LANE_CHUNKS,
        compiler_params=_params("parallel", "arbitrary"),
        name="spectral_multiply",
    )(*z_chunks, kf, plan.f2_r, plan.f2_i, plan.tw_r[:, None, :], plan.tw_i[:, None, :],
      plan.tw_r[:, :, None], plan.tw_i[:, :, None])


def _idft_rows_kernel(a_ref, *refs):
    v_refs, u_refs = refs[:LANE_CHUNKS], refs[LANE_CHUNKS:2 * LANE_CHUNKS]
    bias_ref = refs[2 * LANE_CHUNKS]
    o_refs = refs[2 * LANE_CHUNKS + 1:]
    y = jnp.dot(a_ref[...], _gather_group(v_refs).astype(BF16), preferred_element_type=F32)
    for q, o_ref in enumerate(o_refs):
        bias = bias_ref[:, q * LANES:(q + 1) * LANES]
        for j in range(N2_GROUP):
            lo = j * D_HYENA + q * LANES
            _store_group_row(o_ref, j, y[:, lo:lo + LANES] + _load_group_row(u_refs[q], j) * bias)


def _idft_rows(a3, v_chunks, u, bias):
    p, _, half, n2, _ = u.shape
    _, _, n1, _, _ = v_chunks[0].shape
    groups = n2 // N2_GROUP
    u6 = u.reshape(p, 2, half, groups, N2_GROUP, D_HYENA)
    vblk = pl.BlockSpec((None, 2, n1, None, N2_GROUP, LANES), lambda pi, g: (pi, 0, 0, g, 0, 0))
    yblk = pl.BlockSpec((None, 2, half, None, N2_GROUP, LANES), lambda pi, g: (pi, 0, 0, g, 0, 0))
    outs = pl.pallas_call(
        _idft_rows_kernel,
        grid=(p, groups),
        in_specs=[pl.BlockSpec(a3.shape, lambda pi, g: (0, 0))] + [vblk] * LANE_CHUNKS
                 + _lane_chunk_specs(half) + [pl.BlockSpec((1, D_HYENA), lambda pi, g: (0, 0))],
        out_specs=[yblk] * LANE_CHUNKS,
        out_shape=[jax.ShapeDtypeStruct((p, 2, half, groups, N2_GROUP, LANES), F32)] * LANE_CHUNKS,
        compiler_params=_params("parallel", "parallel"),
        name="idft_rows",
    )(a3, *[v.reshape(p, 2, n1, groups, N2_GROUP, LANES) for v in v_chunks], *([u6] * LANE_CHUNKS), bias)
    return [o.reshape(p * 2 * half * n2, LANES) for o in outs]


def _hyena_long_conv(u, plan, kf, flt_bias):
    b, l, _ = u.shape
    assert b % 2 == 0 and plan.n2 % N2_GROUP == 0
    pairs = u.reshape(b // 2, 2, plan.half, plan.n2, D_HYENA)
    z = _dft_rows(jnp.asarray(plan.a1_data, BF16), pairs, plan.n2)
    v = _spectral_multiply(plan, z, kf)
    return _idft_rows(jnp.asarray(plan.a3, BF16), v, pairs, flt_bias.astype(F32)[None, :])


def _hyena_filter_spectrum(l, plan, z_feat, w1, b1, w2, b2, w3, b3, w4, freq):
    w1p = jnp.pad(w1, ((0, LANES - FILTER_EMB), (0, 0)))
    taps = _filter_taps(l, z_feat, w1p, b1[None, :], w2, b2[None, :], w3, b3[None, :], w4,
                        freq[None, :], _decay_rates())
    z = _dft_rows(jnp.asarray(plan.a1_filt, BF16), taps.reshape(1, 2, plan.half, plan.n2, D_HYENA), plan.n2)
    return _filter_spectrum(plan, z)


def _sigmoid(x):
    return 1.0 / (1.0 + jnp.exp(-x))


def _merge_kernel(h_ref, oa_ref, *refs):
    yh_refs = refs[:LANE_CHUNKS]
    x0_ref, wg_ref, bg_ref, wa_ref, wh_ref, wo_ref, g_ref, b_ref, o_ref = refs[LANE_CHUNKS:]
    h = h_ref[...]
    hb = h.astype(BF16)
    ga = jnp.dot(hb, wg_ref[:, 0:D_MODEL], preferred_element_type=F32) + bg_ref[:, 0:D_MODEL]
    gh = jnp.dot(hb, wg_ref[:, D_MODEL:], preferred_element_type=F32) + bg_ref[:, D_MODEL:]
    y_hyena = jnp.concatenate([r[...] for r in yh_refs], axis=-1)
    o_hyena = (y_hyena * x0_ref[...].astype(F32)).astype(BF16)
    merged = (_sigmoid(ga) * jnp.dot(oa_ref[...], wa_ref[...], preferred_element_type=F32)
              + _sigmoid(gh) * jnp.dot(o_hyena, wh_ref[...], preferred_element_type=F32))
    mix = jnp.dot(merged.astype(BF16), wo_ref[...], preferred_element_type=F32)
    o_ref[...] = _layer_norm(ALPHA * h + mix, g_ref[...], b_ref[...])


def _merge(h, o_attn, y_hyena, x0, wg, bg, wa, wh, wo, ln_g, ln_b):
    n = h.shape[0]
    tm = _row_tile(n, 512)
    row = lambda i: (i, 0)
    fixed = lambda i: (0, 0)
    full = lambda a: pl.BlockSpec(a.shape, fixed)
    return pl.pallas_call(
        _merge_kernel,
        grid=(n // tm,),
        in_specs=[pl.BlockSpec((tm, D_MODEL), row), pl.BlockSpec((tm, D_ATTN), row)]
                 + [pl.BlockSpec((tm, LANES), row)] * LANE_CHUNKS + [pl.BlockSpec((tm, D_HYENA), row)]
                 + [full(a) for a in (wg, bg, wa, wh, wo, ln_g, ln_b)],
        out_specs=pl.BlockSpec((tm, D_MODEL), row),
        out_shape=jax.ShapeDtypeStruct((n, D_MODEL), F32),
        compiler_params=_params("parallel"),
        name="merge",
    )(h, o_attn, *y_hyena, x0, wg, bg, wa, wh, wo, ln_g, ln_b)


ROUTE_TILE = 512
ROW_TILE = 512
EXPERT_ROWS = 512
SLAB = 8
ISSUE_UNROLL = 8
assert SLAB * LANES == D_MODEL
L_E1, L_E2, L_G1, L_G2, L_R1, L_R2 = 0, 1, 2, 3, 4, 5


def _router_kernel(x_ref, w_ref, b_ref, route_ref, route_t_ref, count_ref):
    x, w = x_ref[...], w_ref[...]
    x_hi, w_hi = x.astype(BF16), w.astype(BF16)
    x_lo, w_lo = (x - x_hi.astype(F32)).astype(BF16), (w - w_hi.astype(F32)).astype(BF16)
    logits = (jnp.dot(x_hi, w_hi, preferred_element_type=F32) + jnp.dot(x_lo, w_hi, preferred_element_type=F32)
              + jnp.dot(x_hi, w_lo, preferred_element_type=F32)) + b_ref[...]
    tm = logits.shape[0]
    lane_i = lax.broadcasted_iota(jnp.int32, logits.shape, 1)
    lane = lane_i.astype(F32)
    group_of_lane = jnp.right_shift(lane_i, 2).astype(F32)
    neg = -jnp.inf

    def first_argmax(vals, vmax):
        return jnp.min(jnp.where(vals == vmax, lane, float(LANES)), axis=-1, keepdims=True)

    gl = jnp.where((lane_i >= N_EXPERTS) & (lane_i < N_EXPERTS + N_GROUPS), logits, neg)
    gmax = jnp.max(gl, axis=-1, keepdims=True)
    p_group = 1.0 / jnp.sum(jnp.exp(gl - gmax), axis=-1, keepdims=True)
    g_sel = first_argmax(gl, gmax) - float(N_EXPERTS)
    el = jnp.where((lane_i < N_EXPERTS) & (group_of_lane == g_sel), logits, neg)
    m1 = jnp.max(el, axis=-1, keepdims=True)
    i1 = first_argmax(el, m1)
    el2 = jnp.where(lane == i1, neg, el)
    m2 = jnp.max(el2, axis=-1, keepdims=True)
    i2 = first_argmax(el2, m2)
    e2 = jnp.exp(m2 - m1)
    denom = 1.0 / (1.0 + e2)
    gate1 = denom * p_group
    gate2 = e2 * denom * p_group
    hit1 = lane == i1
    hit2 = lane == i2
    onehot = jnp.where(hit1 | hit2, 1.0, 0.0)
    earlier = (lax.broadcasted_iota(jnp.int32, (tm, tm), 0) > lax.broadcasted_iota(jnp.int32, (tm, tm), 1))
    before = jnp.dot(jnp.where(earlier, 1.0, 0.0).astype(BF16), onehot.astype(BF16), preferred_element_type=F32)
    rank1 = jnp.sum(jnp.where(hit1, before, 0.0), axis=-1, keepdims=True)
    rank2 = jnp.sum(jnp.where(hit2, before, 0.0), axis=-1, keepdims=True)
    out = jnp.zeros_like(logits)
    for l, val in ((L_E1, i1), (L_E2, i2), (L_G1, gate1), (L_G2, gate2), (L_R1, rank1), (L_R2, rank2)):
        out = jnp.where(lane_i == l, val, out)
    route_ref[...] = out
    route_t_ref[...] = out.T[0:8, :]
    count_ref[...] = jnp.broadcast_to(jnp.sum(onehot, axis=0, keepdims=True), count_ref.shape)


def _router(x, w, b):
    n = x.shape[0]
    tm = _row_tile(n, ROUTE_TILE)
    nt = n // tm
    return pl.pallas_call(
        _router_kernel,
        grid=(nt,),
        in_specs=[pl.BlockSpec((tm, D_MODEL), lambda i: (i, 0)),
                  pl.BlockSpec((D_MODEL, LANES), lambda i: (0, 0)),
                  pl.BlockSpec((1, LANES), lambda i: (0, 0))],
        out_specs=[pl.BlockSpec((tm, LANES), lambda i: (i, 0)), pl.BlockSpec((8, tm), lambda i: (0, i)),
                   pl.BlockSpec((None, 8, LANES), lambda i: (i, 0, 0))],
        out_shape=[jax.ShapeDtypeStruct((n, LANES), F32), jax.ShapeDtypeStruct((8, n), F32),
                   jax.ShapeDtypeStruct((nt, 8, LANES), F32)],
        compiler_params=_params("parallel"),
        name="router",
    )(x, w, b)


def _routing_tables(route_t, counts, n):
    tm = _row_tile(n, ROUTE_TILE)
    nt = n // tm
    cnt = counts[:, 0, :N_EXPERTS].astype(jnp.int32)
    total = jnp.sum(cnt, axis=0)
    padded = (total + EXPERT_ROWS - 1) // EXPERT_ROWS * EXPERT_ROWS
    ends = jnp.cumsum(padded)
    tile_base = (ends - padded)[None, :] + jnp.cumsum(cnt, axis=0) - cnt
    base_tok = jnp.broadcast_to(tile_base.T[:, :, None], (N_EXPERTS, nt, tm)).reshape(N_EXPERTS, n)
    experts = jnp.arange(N_EXPERTS, dtype=jnp.int32)[:, None]
    route_i = route_t.astype(jnp.int32)

    def position(l_e, l_r):
        return jnp.sum(jnp.where(route_i[l_e][None, :] == experts, base_tok, 0), axis=0) + route_i[l_r]

    pos = jnp.stack([position(L_E1, L_R1), position(L_E2, L_R2)], axis=1).reshape(2 * n)
    n_blocks = 2 * n // EXPERT_ROWS + N_EXPERTS
    block_start = jnp.arange(n_blocks, dtype=jnp.int32) * EXPERT_ROWS
    block_expert = jnp.sum((ends[None, :] <= block_start[:, None]).astype(jnp.int32), axis=1)
    block_valid = (block_expert < N_EXPERTS).astype(jnp.int32)
    last_block = jnp.where(padded > total, ends // EXPERT_ROWS - 1, -1)
    tail = ends[-1] // EXPERT_ROWS + jnp.arange(N_EXPERTS, dtype=jnp.int32)
    zero_blocks = jnp.concatenate([last_block, jnp.where(tail < n_blocks, tail, -1)]).astype(jnp.int32)
    return pos, jnp.minimum(block_expert, N_EXPERTS - 1), block_valid, zero_blocks, n_blocks


def _to_slabs(slab_ref, base, x):
    rows = x.shape[0]
    for s in range(SLAB):
        slab_ref[pl.ds(base + s, rows, stride=SLAB), :] = x[:, s * LANES:(s + 1) * LANES]


def _from_slabs(slab_ref, base, rows):
    return jnp.concatenate([slab_ref[pl.ds(base + s, rows, stride=SLAB), :] for s in range(SLAB)], axis=-1)


def _dispatch_kernel(pos_ref, zb_ref, x_ref, xs_ref, slab_a, slab_b, sems):
    tm = x_ref.shape[0] // 2
    step = pl.program_id(0)

    def tile_wait(slab_ref, sem):
        for _ in range(2):
            pltpu.make_async_copy(slab_ref, xs_ref.at[pl.ds(0, tm * SLAB)], sem).wait()

    @pl.when(step == 0)
    def _():
        slab_a[...] = jnp.zeros_like(slab_a)
        block = slab_a.shape[0]
        for j in range(zb_ref.shape[0]):
            @pl.when(zb_ref[j] >= 0)
            def _():
                pltpu.make_async_copy(slab_a, xs_ref.at[pl.ds(zb_ref[j] * block, block)], sems.at[0]).start()
        for j in range(zb_ref.shape[0]):
            @pl.when(zb_ref[j] >= 0)
            def _():
                pltpu.make_async_copy(slab_a, xs_ref.at[pl.ds(0, block)], sems.at[0]).wait()

    for half, slab_ref in enumerate((slab_a, slab_b)):
        sem = sems.at[half]

        @pl.when(step > 0)
        def _():
            tile_wait(slab_ref, sem)

        _to_slabs(slab_ref, 0, x_ref[half * tm:(half + 1) * tm, :])

        def issue(t, carry):
            for k in range(2):
                p = pos_ref[2 * (half * tm + t) + k]
                pltpu.make_async_copy(slab_ref.at[pl.ds(t * SLAB, SLAB)], xs_ref.at[pl.ds(p * SLAB, SLAB)],
                                      sem).start(priority=k)
            return carry

        lax.fori_loop(0, tm, issue, 0, unroll=ISSUE_UNROLL)

    @pl.when(step == pl.num_programs(0) - 1)
    def _():
        tile_wait(slab_a, sems.at[0])
        tile_wait(slab_b, sems.at[1])


def _dispatch(x, pos, zero_blocks, n_rows):
    n = x.shape[0]
    tm = _row_tile(n, ROW_TILE)
    assert tm == EXPERT_ROWS
    assert n % (2 * tm) == 0
    return pl.pallas_call(
        _dispatch_kernel,
        grid=(n // (2 * tm),),
        in_specs=[pl.BlockSpec((4 * tm,), lambda i: (i,), memory_space=pltpu.SMEM),
                  pl.BlockSpec(zero_blocks.shape, lambda i: (0,), memory_space=pltpu.SMEM),
                  pl.BlockSpec((2 * tm, D_MODEL), lambda i: (i, 0))],
        out_specs=pl.BlockSpec(memory_space=pl.ANY),
        out_shape=jax.ShapeDtypeStruct((n_rows * SLAB, LANES), F32),
        scratch_shapes=[pltpu.VMEM((tm * SLAB, LANES), F32), pltpu.VMEM((tm * SLAB, LANES), F32),
                        pltpu.SemaphoreType.DMA((2,))],
        compiler_params=_params("arbitrary"),
        name="dispatch",
    )(pos, zero_blocks, x)


def _expert_ffn_kernel(be_ref, bv_ref, xs_ref, w1_ref, w3_ref, w2_ref, y_ref):
    rows = xs_ref.shape[0] // SLAB
    valid = bv_ref[pl.program_id(0)] > 0

    @pl.when(valid)
    def _():
        xb = _from_slabs(xs_ref, 0, rows).astype(BF16)
        a = jnp.dot(xb, w1_ref[...], preferred_element_type=F32)
        g = jnp.dot(xb, w3_ref[...], preferred_element_type=F32)
        hidden = (a * _sigmoid(a) * g).astype(BF16)
        _to_slabs(y_ref, 0, jnp.dot(hidden, w2_ref[...], preferred_element_type=F32))

    @pl.when(jnp.logical_not(valid))
    def _():
        y_ref[...] = jnp.zeros_like(y_ref)


def _expert_ffn(xs, block_expert, block_valid, n_blocks, w1, w3, w2):
    rows = EXPERT_ROWS * SLAB
    blk = lambda b, be, bv: (b, 0)
    wsel = lambda b, be, bv: (be[b], 0, 0)
    return pl.pallas_call(
        _expert_ffn_kernel,
        grid_spec=pltpu.PrefetchScalarGridSpec(
            num_scalar_prefetch=2,
            grid=(n_blocks,),
            in_specs=[pl.BlockSpec((rows, LANES), blk),
                      pl.BlockSpec((None, D_MODEL, D_EXPERT), wsel),
                      pl.BlockSpec((None, D_MODEL, D_EXPERT), wsel),
                      pl.BlockSpec((None, D_EXPERT, D_MODEL), wsel)],
            out_specs=pl.BlockSpec((rows, LANES), blk)),
        out_shape=jax.ShapeDtypeStruct(xs.shape, F32),
        compiler_params=_params("arbitrary"),
        name="expert_ffn",
    )(block_expert, block_valid, xs, w1, w3, w2)


def _combine_kernel(pos_ref, pos_next_ref, x_ref, route_ref, g_ref, b_ref, y_ref, o_ref, slab_a, slab_b, sems):
    tm = x_ref.shape[0] // 2
    step = pl.program_id(0)

    def fetch(tile_pos_ref, half, slab_ref, sem):
        def issue(t, carry):
            for k in range(2):
                p = tile_pos_ref[2 * (half * tm + t) + k]
                pltpu.make_async_copy(y_ref.at[pl.ds(p * SLAB, SLAB)],
                                      slab_ref.at[pl.ds((k * tm + t) * SLAB, SLAB)], sem).start(priority=k)
            return carry

        lax.fori_loop(0, tm, issue, 0, unroll=ISSUE_UNROLL)

    def finish(half, slab_ref, sem):
        for k in range(2):
            pltpu.make_async_copy(y_ref.at[pl.ds(0, tm * SLAB)],
                                  slab_ref.at[pl.ds(k * tm * SLAB, tm * SLAB)], sem).wait()
        rows = slice(half * tm, (half + 1) * tm)
        route = route_ref[rows, :]
        ffn = (route[:, L_G1:L_G1 + 1] * _from_slabs(slab_ref, 0, tm)
               + route[:, L_G2:L_G2 + 1] * _from_slabs(slab_ref, tm * SLAB, tm))
        o_ref[rows, :] = _layer_norm(ALPHA * x_ref[rows, :] + ffn, g_ref[...], b_ref[...])

    @pl.when(step == 0)
    def _():
        fetch(pos_ref, 0, slab_a, sems.at[0])

    fetch(pos_ref, 1, slab_b, sems.at[1])
    finish(0, slab_a, sems.at[0])

    @pl.when(step < pl.num_programs(0) - 1)
    def _():
        fetch(pos_next_ref, 0, slab_a, sems.at[0])

    finish(1, slab_b, sems.at[1])


def _combine(x, route, pos, y, ln_g, ln_b):
    n = x.shape[0]
    tm = _row_tile(n, ROW_TILE)
    assert n % (2 * tm) == 0
    steps = n // (2 * tm)
    row = lambda i: (i, 0)
    fixed = lambda i: (0, 0)
    return pl.pallas_call(
        _combine_kernel,
        grid=(steps,),
        in_specs=[pl.BlockSpec((4 * tm,), lambda i: (i,), memory_space=pltpu.SMEM),
                  pl.BlockSpec((4 * tm,), lambda i: (jnp.minimum(i + 1, steps - 1),), memory_space=pltpu.SMEM),
                  pl.BlockSpec((2 * tm, D_MODEL), row), pl.BlockSpec((2 * tm, LANES), row),
                  pl.BlockSpec((1, D_MODEL), fixed), pl.BlockSpec((1, D_MODEL), fixed),
                  pl.BlockSpec(memory_space=pl.ANY)],
        out_specs=pl.BlockSpec((2 * tm, D_MODEL), row),
        out_shape=jax.ShapeDtypeStruct((n, D_MODEL), F32),
        scratch_shapes=[pltpu.VMEM((2 * tm * SLAB, LANES), F32), pltpu.VMEM((2 * tm * SLAB, LANES), F32),
                        pltpu.SemaphoreType.DMA((2,))],
        compiler_params=_params("arbitrary"),
        name="combine",
    )(pos, pos, x, route, ln_g, ln_b, y)


def _moe(x, lay):
    n = x.shape[0]
    route, route_t, counts = _router(x, lay["w_route"], lay["b_route"])
    pos, block_expert, block_valid, zero_blocks, n_blocks = _routing_tables(route_t, counts, n)
    xs = _dispatch(x, pos, zero_blocks, n_blocks * EXPERT_ROWS)
    y = _expert_ffn(xs, block_expert, block_valid, n_blocks, lay["w1"], lay["w3"], lay["w2"])
    return _combine(x, route, pos, y, lay["ln2_g"], lay["ln2_b"])


def _prepare_layer(i, w_in, b_in, short_w, short_b, na_rpb, w_branch_attn, w_branch_hyena, w_out,
                   ln1_g, ln1_b, w_group, b_group, w_router, b_router, w1, w3, w2, ln2_g, ln2_b, flt_bias):
    row = lambda a: a[None, :].astype(F32)
    w_route = jnp.zeros((D_MODEL, LANES), F32)
    w_route = w_route.at[:, 0:N_EXPERTS].set(w_router[i]).at[:, N_EXPERTS:N_EXPERTS + N_GROUPS].set(w_group[i])
    b_route = jnp.zeros((1, LANES), F32)
    b_route = b_route.at[0, 0:N_EXPERTS].set(b_router[i]).at[0, N_EXPERTS:N_EXPERTS + N_GROUPS].set(b_group[i])
    return dict(
        w_qkvh=w_in[i][:, :D_QKVH].astype(BF16), b_qkvh=row(b_in[i][:D_QKVH]),
        w_gates=w_in[i][:, D_QKVH:].astype(BF16), b_gates=row(b_in[i][D_QKVH:]),
        short_w=short_w[i], short_b=row(short_b[i]), flt_bias=flt_bias[i],
        na_bias=_na_bias_table(na_rpb[i]),
        wa=w_branch_attn[i].astype(BF16), wh=w_branch_hyena[i].astype(BF16), wo=w_out[i].astype(BF16),
        ln1_g=row(ln1_g[i]), ln1_b=row(ln1_b[i]), ln2_g=row(ln2_g[i]), ln2_b=row(ln2_b[i]),
        w_route=w_route, b_route=b_route,
        w1=w1[i].astype(BF16), w3=w3[i].astype(BF16), w2=w2[i].astype(BF16),
    )


def _trunk(x, ln_in_g, ln_in_b, layers, filt):
    b, l, d = x.shape
    n = b * l
    plan = _DftPlan(l)
    z_feat = _filter_features(l)
    head_mask = jnp.asarray(_head_mask(), BF16)
    h = x.reshape(n, d)
    for i, lay in enumerate(layers):
        outs = _inproj(h, l, ln_in_g[None, :], ln_in_b[None, :], lay["w_qkvh"], lay["b_qkvh"],
                       lay["short_w"], lay["short_b"], apply_ln=(i == 0))
        if i == 0:
            h, q, k, v, u, x0 = outs
        else:
            q, k, v, u, x0 = outs
        seq = lambda a: a.reshape(b, l, a.shape[-1])
        o_attn = _neighbourhood_attention(seq(q), seq(k), seq(v), lay["na_bias"], head_mask).reshape(n, D_ATTN)
        kf = _hyena_filter_spectrum(l, plan, z_feat, *[f[i] for f in filt])
        y_hyena = _hyena_long_conv(seq(u), plan, kf, lay["flt_bias"])
        h = _merge(h, o_attn, y_hyena, x0, lay["w_gates"], lay["b_gates"], lay["wa"], lay["wh"], lay["wo"],
                   lay["ln1_g"], lay["ln1_b"])
        h = _moe(h, lay)
    return h.reshape(b, l, d)


def kernel(x_prompt, x_sample, ln_in_g, ln_in_b, w_in, b_in, short_w, short_b, na_rpb, flt_w1, flt_b1, flt_w2, flt_b2, flt_w3, flt_b3, flt_w4, flt_freq, flt_bias, w_branch_attn, w_branch_hyena, w_out, ln1_g, ln1_b, w_group, b_group, w_router, b_router, w1, w3, w2, ln2_g, ln2_b):
    layers = [_prepare_layer(i, w_in, b_in, short_w, short_b, na_rpb, w_branch_attn, w_branch_hyena, w_out,
                             ln1_g, ln1_b, w_group, b_group, w_router, b_router, w1, w3, w2, ln2_g, ln2_b,
                             flt_bias) for i in range(DEPTH)]
    filt = (flt_w1, flt_b1, flt_w2, flt_b2, flt_w3, flt_b3, flt_w4, flt_freq)
    y_prompt = _trunk(x_prompt, ln_in_g, ln_in_b, layers, filt)
    y_sample = _trunk(x_sample, ln_in_g, ln_in_b, layers, filt)
    return (y_prompt, y_sample)
```
